```python
import jax, jax.numpy as jnp
from jax import lax
import numpy as np

D_MODEL = 2048
BATCH = 1
SEQ = 8192
DEPTH = 1
DEC_BATCH = 4
DEC_SEQ = 2048
PAST_LEN = 128

N_MEM = 256
GRID_W = 64
BLOCK_Q = 128
ROPE_THETA = 10000.0
EPS = 1e-6
HEADS_A = 8
KV_HEADS_A = 2
HEAD_DIM_A = 128
HEADS_B = 8
Q_LORA_B = 512
KV_LORA_B = 256
NOPE_DIM_B = 128
ROPE_DIM_B = 64
V_DIM_B = 128
HEADS_M = 4
HEAD_DIM_M = 128
N_BRANCH = 3
D_FF = 4 * D_MODEL
IN_WIDTHS = (
    HEADS_A * HEAD_DIM_A,
    KV_HEADS_A * HEAD_DIM_A,
    KV_HEADS_A * HEAD_DIM_A,
    Q_LORA_B,
    KV_LORA_B,
    ROPE_DIM_B,
    HEADS_M * HEAD_DIM_M,
    N_BRANCH * D_MODEL,
)
IN_WIDTH = (HEADS_A * HEAD_DIM_A + 2 * KV_HEADS_A * HEAD_DIM_A + Q_LORA_B + KV_LORA_B
            + ROPE_DIM_B + HEADS_M * HEAD_DIM_M + N_BRANCH * D_MODEL)

kernel_name = "hybrid_gqa_mla_memory_encoder"


def rmsnorm(x, g):
    x32 = x.astype(jnp.float32)
    y = x32 * lax.rsqrt(jnp.mean(x32 * x32, axis=-1, keepdims=True) + EPS)
    return (y * g.astype(jnp.float32)).astype(x.dtype)


def split_cols(z, widths):
    outs, start = [], 0
    for w in widths:
        outs.append(z[..., start:start + w])
        start += w
    return outs


def grid_positions(n_tokens):
    n_rows = n_tokens // GRID_W
    rows = jnp.repeat(jnp.arange(n_rows, dtype=jnp.int32), GRID_W)
    cols = jnp.tile(jnp.arange(GRID_W, dtype=jnp.int32), n_rows)
    return rows, cols


def rope_cos_sin(pos, dim):
    inv_freq = ROPE_THETA ** (-jnp.arange(0, dim, 2, dtype=jnp.float32) / dim)
    ang = pos.astype(jnp.float32)[:, None] * inv_freq[None, :]
    return jnp.cos(ang), jnp.sin(ang)


def apply_rope_1d(x, cos, sin):
    half = x.shape[-1] // 2
    x1, x2 = x[..., :half], x[..., half:]
    c = cos[:, None, :].astype(x.dtype)
    s = sin[:, None, :].astype(x.dtype)
    return jnp.concatenate([x1 * c - x2 * s, x2 * c + x1 * s], axis=-1)


def axial_rope(x, rows, cols):
    h = x.shape[-1] // 2
    cr, sr = rope_cos_sin(rows, h)
    cc, sc = rope_cos_sin(cols, h)
    return jnp.concatenate([apply_rope_1d(x[..., :h], cr, sr),
                            apply_rope_1d(x[..., h:], cc, sc)], axis=-1)


def blockwise_attention(q, k, v):
    b, sq, hk, g, dq = q.shape
    dv = v.shape[-1]
    nb = sq // BLOCK_Q
    scale = dq ** -0.5
    qb = q.reshape(b, nb, BLOCK_Q, hk, g, dq).transpose(1, 0, 2, 3, 4, 5)

    def attend(q_blk):
        s = jnp.einsum("bqhgd,bkhd->bhgqk", q_blk, k).astype(jnp.float32) * scale
        p = jax.nn.softmax(s, axis=-1).astype(v.dtype)
        return jnp.einsum("bhgqk,bkhd->bqhgd", p, v)

    o = lax.map(attend, qb)
    return o.transpose(1, 0, 2, 3, 4, 5).reshape(b, sq, hk * g * dv)


def mixer_block(x, mem, rows, cols, g_mix, w_in, g_qa, g_ka, g_cq, w_q_b, g_ckv, w_kv_b,
                g_mem, w_mem_kv, w_br_a, w_br_b, w_br_m, w_out):
    b, s, _ = x.shape
    n = rmsnorm(x, g_mix)
    z = n @ w_in
    q_a, k_a, v_a, c_q, c_kv, k_rope, q_m, gate_logits = split_cols(z, IN_WIDTHS)

    q_a = axial_rope(rmsnorm(q_a.reshape(b, s, HEADS_A, HEAD_DIM_A), g_qa), rows, cols)
    k_a = axial_rope(rmsnorm(k_a.reshape(b, s, KV_HEADS_A, HEAD_DIM_A), g_ka), rows, cols)
    v_a = v_a.reshape(b, s, KV_HEADS_A, HEAD_DIM_A)
    q_a = q_a.reshape(b, s, KV_HEADS_A, HEADS_A // KV_HEADS_A, HEAD_DIM_A)
    o_a = blockwise_attention(q_a, k_a, v_a)

    q_b = (rmsnorm(c_q, g_cq) @ w_q_b).reshape(b, s, HEADS_B, NOPE_DIM_B + ROPE_DIM_B)
    q_b = jnp.concatenate([q_b[..., :NOPE_DIM_B],
                           axial_rope(q_b[..., NOPE_DIM_B:], rows, cols)], axis=-1)
    kv_b = (rmsnorm(c_kv, g_ckv) @ w_kv_b).reshape(b, s, HEADS_B, NOPE_DIM_B + V_DIM_B)
    k_nope, v_b = kv_b[..., :NOPE_DIM_B], kv_b[..., NOPE_DIM_B:]
    k_pe = axial_rope(k_rope.reshape(b, s, 1, ROPE_DIM_B), rows, cols)
    k_b = jnp.concatenate([k_nope, jnp.broadcast_to(k_pe, (b, s, HEADS_B, ROPE_DIM_B))], axis=-1)
    o_b = blockwise_attention(q_b[:, :, :, None, :], k_b, v_b)

    n_mem = mem.shape[1]
    kv_m = (rmsnorm(mem, g_mem) @ w_mem_kv).reshape(b, n_mem, 2, HEADS_M, HEAD_DIM_M)
    o_m = blockwise_attention(q_m.reshape(b, s, HEADS_M, 1, HEAD_DIM_M),
                              kv_m[:, :, 0], kv_m[:, :, 1])

    gates = jax.nn.sigmoid(gate_logits.astype(jnp.float32)).astype(x.dtype)
    gates = gates.reshape(b, s, N_BRANCH, D_MODEL)
    merged = (gates[:, :, 0] * (o_a @ w_br_a)
              + gates[:, :, 1] * (o_b @ w_br_b)
              + gates[:, :, 2] * (o_m @ w_br_m))
    return merged @ w_out


def encoder_trunk(x, mem, g_mix, w_in, g_qa, g_ka, g_cq, w_q_b, g_ckv, w_kv_b, g_mem,
                  w_mem_kv, w_br_a, w_br_b, w_br_m, w_out, g_mlp, w_up, w_down, g_final):
    rows, cols = grid_positions(x.shape[1])
    h = x
    for l in range(DEPTH):
        h = h + mixer_block(h, mem, rows, cols, g_mix[l], w_in[l], g_qa[l], g_ka[l], g_cq[l],
                            w_q_b[l], g_ckv[l], w_kv_b[l], g_mem[l], w_mem_kv[l],
                            w_br_a[l], w_br_b[l], w_br_m[l], w_out[l])
        n = rmsnorm(h, g_mlp[l])
        h = h + jnp.square(jax.nn.relu(n @ w_up[l])) @ w_down[l]
    return rmsnorm(h, g_final)


def setup_inputs(seed: int = 0) -> dict:
    key = jax.random.key(seed)
    ks = jax.random.split(key, 24)
    f32 = jnp.float32

    def w(k, fan_in, fan_out):
        return jax.random.normal(k, (DEPTH, fan_in, fan_out), f32) * fan_in ** -0.5

    def gain(k, dim):
        return 1.0 + 0.02 * jax.random.normal(k, (DEPTH, dim), f32)

    return {
        "x_prompt": jax.random.normal(ks[0], (BATCH, SEQ, D_MODEL), f32),
        "x_sample": jax.random.normal(ks[1], (DEC_BATCH, DEC_SEQ, D_MODEL), f32),
        "mem_prompt": jax.random.normal(ks[2], (BATCH, N_MEM, D_MODEL), f32),
        "mem_sample": jax.random.normal(ks[3], (DEC_BATCH, N_MEM, D_MODEL), f32),
        "g_mix": gain(ks[4], D_MODEL),
        "w_in": w(ks[5], D_MODEL, IN_WIDTH),
        "g_qa": gain(ks[6], HEAD_DIM_A),
        "g_ka": gain(ks[7], HEAD_DIM_A),
        "g_cq": gain(ks[8], Q_LORA_B),
        "w_q_b": w(ks[9], Q_LORA_B, HEADS_B * (NOPE_DIM_B + ROPE_DIM_B)),
        "g_ckv": gain(ks[10], KV_LORA_B),
        "w_kv_b": w(ks[11], KV_LORA_B, HEADS_B * (NOPE_DIM_B + V_DIM_B)),
        "g_mem": gain(ks[12], D_MODEL),
        "w_mem_kv": w(ks[13], D_MODEL, 2 * HEADS_M * HEAD_DIM_M),
        "w_br_a": w(ks[14], HEADS_A * HEAD_DIM_A, D_MODEL),
        "w_br_b": w(ks[15], HEADS_B * V_DIM_B, D_MODEL),
        "w_br_m": w(ks[16], HEADS_M * HEAD_DIM_M, D_MODEL),
        "w_out": w(ks[17], D_MODEL, D_MODEL),
        "g_mlp": gain(ks[18], D_MODEL),
        "w_up": w(ks[19], D_MODEL, D_FF),
        "w_down": w(ks[20], D_FF, D_MODEL),
        "g_final": 1.0 + 0.02 * jax.random.normal(ks[21], (D_MODEL,), f32),
    }


def reference(x_prompt, x_sample, mem_prompt, mem_sample, g_mix, w_in, g_qa, g_ka, g_cq,
              w_q_b, g_ckv, w_kv_b, g_mem, w_mem_kv, w_br_a, w_br_b, w_br_m, w_out,
              g_mlp, w_up, w_down, g_final):
    weights = (g_mix, w_in, g_qa, g_ka, g_cq, w_q_b, g_ckv, w_kv_b, g_mem, w_mem_kv,
               w_br_a, w_br_b, w_br_m, w_out, g_mlp, w_up, w_down, g_final)
    y_prompt = encoder_trunk(x_prompt, mem_prompt, *weights)
    y_sample = encoder_trunk(x_sample, mem_sample, *weights)
    return (y_prompt, y_sample)
```

```python
import functools
import math

import jax
import jax.numpy as jnp
from jax import lax
from jax.experimental import pallas as pl
from jax.experimental.pallas import tpu as pltpu

F32 = jnp.float32
BF16 = jnp.bfloat16

EPS = 1e-6
ROPE_THETA = 10000.0
GRID_W = 64
LOG2E = math.log2(math.e)

HEADS_A, KV_HEADS_A, HEAD_DIM_A = 8, 2, 128
HEADS_B, Q_LORA_B, KV_LORA_B = 8, 512, 256
NOPE_DIM_B, ROPE_DIM_B, V_DIM_B = 128, 64, 128
HEADS_M, HEAD_DIM_M = 4, 128
N_BRANCH = 3
LANES = 128
QB_PAD = 256

VMEM_LIMIT = 56 * 1024 * 1024


def _cparams(sem):
    return pltpu.CompilerParams(dimension_semantics=sem, vmem_limit_bytes=VMEM_LIMIT)


def _rms(x, g):
    y = x * lax.rsqrt(jnp.mean(x * x, axis=-1, keepdims=True) + EPS)
    return y * g


def _dot(a, b):
    return jnp.dot(a, b, preferred_element_type=F32)


def _rope(x, cos, sin_dn, sin_up, half):
    return (x * cos + pltpu.roll(x, LANES - half, 1) * sin_dn
            + pltpu.roll(x, half, 1) * sin_up)


def _norm_kernel(x_ref, g_ref, o_ref):
    o_ref[...] = _rms(x_ref[...], g_ref[...]).astype(o_ref.dtype)


def _norm(x, g, tm=512):
    t, d = x.shape
    return pl.pallas_call(
        _norm_kernel,
        grid=(t // tm,),
        in_specs=[pl.BlockSpec((tm, d), lambda i: (i, 0)),
                  pl.BlockSpec((1, d), lambda i: (0, 0))],
        out_specs=pl.BlockSpec((tm, d), lambda i: (i, 0)),
        out_shape=jax.ShapeDtypeStruct((t, d), BF16),
        compiler_params=_cparams(("parallel",)),
        name="norm",
    )(x, g)


def _qkv_a_kernel(n_ref, w_ref, gq_ref, gk_ref, cos_ref, sdn_ref, sup_ref, o_ref, *, q_scale):
    j = pl.program_id(1)
    z = _dot(n_ref[...], w_ref[...])
    cos, sdn, sup = cos_ref[...], sdn_ref[...], sup_ref[...]
    n_heads = z.shape[1] // LANES

    def normed_rope(h, g):
        zh = z[:, h * LANES:(h + 1) * LANES]
        return _rope(_rms(zh, g), cos, sdn, sup, HEAD_DIM_A // 4)

    @pl.when(j < 2)
    def _():
        for h in range(n_heads):
            o_ref[:, h * LANES:(h + 1) * LANES] = (normed_rope(h, gq_ref[...]) * q_scale).astype(o_ref.dtype)

    @pl.when(j == 2)
    def _():
        for h in range(KV_HEADS_A):
            o_ref[:, h * LANES:(h + 1) * LANES] = normed_rope(h, gk_ref[...]).astype(o_ref.dtype)
        o_ref[:, KV_HEADS_A * LANES:] = z[:, KV_HEADS_A * LANES:].astype(o_ref.dtype)


def _qkv_a(n, w, gq, gk, tabs, seq, tm=1024, tn=512):
    t, d = n.shape
    nn = w.shape[1]
    cos, sdn, sup = tabs
    tm = min(tm, seq)
    nblk = seq // tm
    tab_spec = pl.BlockSpec((tm, LANES), lambda i, j: (i % nblk, 0))
    vec_spec = pl.BlockSpec((1, LANES), lambda i, j: (0, 0))
    return pl.pallas_call(
        functools.partial(_qkv_a_kernel, q_scale=HEAD_DIM_A ** -0.5 * LOG2E),
        grid=(t // tm, nn // tn),
        in_specs=[pl.BlockSpec((tm, d), lambda i, j: (i, 0)),
                  pl.BlockSpec((d, tn), lambda i, j: (0, j)),
                  vec_spec, vec_spec, tab_spec, tab_spec, tab_spec],
        out_specs=pl.BlockSpec((tm, tn), lambda i, j: (i, j)),
        out_shape=jax.ShapeDtypeStruct((t, nn), BF16),
        compiler_params=_cparams(("parallel", "arbitrary")),
        name="qkv_a",
    )(n, w, gq, gk, cos, sdn, sup)


def _latent_kernel(n_ref, wl_ref, gcq_ref, gckv_ref, wq_ref, wkv_ref, cos_ref, sdn_ref, sup_ref,
                   q_ref, k_ref, v_ref, *, q_scale):
    lat = _dot(n_ref[...], wl_ref[...])
    cos, sdn, sup = cos_ref[...], sdn_ref[...], sup_ref[...]
    half = ROPE_DIM_B // 4
    cq = _rms(lat[:, :Q_LORA_B], gcq_ref[...]).astype(BF16)
    qb = _dot(cq, wq_ref[...])
    for h in range(HEADS_B):
        c0 = h * QB_PAD
        q_ref[:, c0:c0 + LANES] = (qb[:, c0:c0 + LANES] * q_scale).astype(q_ref.dtype)
        q_ref[:, c0 + LANES:c0 + QB_PAD] = (
            _rope(qb[:, c0 + LANES:c0 + QB_PAD], cos, sdn, sup, half) * q_scale).astype(q_ref.dtype)
    ckv = _rms(lat[:, Q_LORA_B:Q_LORA_B + KV_LORA_B], gckv_ref[...]).astype(BF16)
    kvb = _dot(ckv, wkv_ref[...])
    kpe = _rope(lat[:, Q_LORA_B + KV_LORA_B:], cos, sdn, sup, half).astype(k_ref.dtype)
    for h in range(HEADS_B):
        c0 = h * QB_PAD
        k_ref[:, c0:c0 + LANES] = kvb[:, h * LANES:(h + 1) * LANES].astype(k_ref.dtype)
        k_ref[:, c0 + LANES:c0 + QB_PAD] = kpe
    v_ref[...] = kvb[:, HEADS_B * NOPE_DIM_B:].astype(v_ref.dtype)


def _latent(n, wl, gcq, gckv, wq, wkv, tabs, seq, tm=256):
    t, d = n.shape
    cos, sdn, sup = tabs
    nblk = seq // tm
    const = lambda i: (0, 0)
    tab_spec = pl.BlockSpec((tm, LANES), lambda i: (i % nblk, 0))
    row = lambda w: pl.BlockSpec((tm, w), lambda i: (i, 0))
    q_scale = (NOPE_DIM_B + ROPE_DIM_B) ** -0.5 * LOG2E
    return pl.pallas_call(
        functools.partial(_latent_kernel, q_scale=q_scale),
        grid=(t // tm,),
        in_specs=[row(d),
                  pl.BlockSpec(wl.shape, const), pl.BlockSpec(gcq.shape, const),
                  pl.BlockSpec(gckv.shape, const), pl.BlockSpec(wq.shape, const),
                  pl.BlockSpec(wkv.shape, const), tab_spec, tab_spec, tab_spec],
        out_specs=[row(HEADS_B * QB_PAD), row(HEADS_B * QB_PAD), row(HEADS_B * V_DIM_B)],
        out_shape=[jax.ShapeDtypeStruct((t, HEADS_B * QB_PAD), BF16),
                   jax.ShapeDtypeStruct((t, HEADS_B * QB_PAD), BF16),
                   jax.ShapeDtypeStruct((t, HEADS_B * V_DIM_B), BF16)],
        compiler_params=_cparams(("parallel",)),
        name="latent",
    )(n, wl, gcq, gckv, wq, wkv, cos, sdn, sup)


def _gate_kernel(n_ref, w_ref, o_ref, *, n_gate_blocks, q_scale):
    j = pl.program_id(1)
    z = _dot(n_ref[...], w_ref[...])

    @pl.when(j < n_gate_blocks)
    def _():
        o_ref[...] = jax.nn.sigmoid(z).astype(o_ref.dtype)

    @pl.when(j >= n_gate_blocks)
    def _():
        o_ref[...] = (z * q_scale).astype(o_ref.dtype)


def _gates_qm(n, w, n_gate_cols, tm=1024, tn=512):
    t, d = n.shape
    nn = w.shape[1]
    return pl.pallas_call(
        functools.partial(_gate_kernel, n_gate_blocks=n_gate_cols // tn,
                          q_scale=HEAD_DIM_M ** -0.5 * LOG2E),
        grid=(t // tm, nn // tn),
        in_specs=[pl.BlockSpec((tm, d), lambda i, j: (i, 0)),
                  pl.BlockSpec((d, tn), lambda i, j: (0, j))],
        out_specs=pl.BlockSpec((tm, tn), lambda i, j: (i, j)),
        out_shape=jax.ShapeDtypeStruct((t, nn), BF16),
        compiler_params=_cparams(("parallel", "arbitrary")),
        name="gates_qm",
    )(n, w)


def _mem_kv_kernel(m_ref, g_ref, w_ref, o_ref):
    nm = _rms(m_ref[...], g_ref[...]).astype(BF16)
    o_ref[...] = _dot(nm, w_ref[...]).astype(o_ref.dtype)


def _mem_kv(mem, g, w, tm=256):
    t, d = mem.shape
    nn = w.shape[1]
    return pl.pallas_call(
        _mem_kv_kernel,
        grid=(t // tm,),
        in_specs=[pl.BlockSpec((tm, d), lambda i: (i, 0)),
                  pl.BlockSpec((1, d), lambda i: (0, 0)),
                  pl.BlockSpec((d, nn), lambda i: (0, 0))],
        out_specs=pl.BlockSpec((tm, nn), lambda i: (i, 0)),
        out_shape=jax.ShapeDtypeStruct((t, nn), BF16),
        compiler_params=_cparams(("parallel",)),
        name="mem_kv",
    )(mem, g, w)


def _flash_kernel(q_ref, k_ref, v_ref, o_ref, *, group, dq, dv, tk):
    tq = q_ref.shape[0]
    nk = k_ref.shape[0] // tk
    if group > 1:
        q = jnp.concatenate([q_ref[:, g * dq:(g + 1) * dq] for g in range(group)], axis=0)
    else:
        q = q_ref[...]
    rows = group * tq

    def step(j, carry):
        m, l, acc = carry
        off = pl.multiple_of(j * tk, tk)
        ks = k_ref[pl.ds(off, tk), :]
        vs = v_ref[pl.ds(off, tk), :]
        s = lax.dot_general(q, ks, (((1,), (1,)), ((), ())), preferred_element_type=F32)
        m_new = jnp.maximum(m, jnp.max(s, axis=-1, keepdims=True))
        alpha = jnp.exp2(m - m_new)
        p = jnp.exp2(s - m_new)
        l_new = alpha * l + jnp.sum(p, axis=-1, keepdims=True)
        acc_new = alpha * acc + _dot(p.astype(BF16), vs)
        return m_new, l_new, acc_new

    init = (jnp.full((rows, 1), -jnp.inf, F32), jnp.zeros((rows, 1), F32),
            jnp.zeros((rows, dv), F32))
    if nk == 1:
        _, l, acc = step(0, init)
    else:
        _, l, acc = lax.fori_loop(0, nk, step, init)
    o = acc / l
    for g in range(group):
        o_ref[:, g * dv:(g + 1) * dv] = o[g * tq:(g + 1) * tq].astype(o_ref.dtype)


def _flash(q, k, v, *, n_kv_heads, group, dq, dv, q_col, k_col, v_col, tq, tk):
    b, s, _ = q.shape
    sk = k.shape[1]
    tk = min(tk, sk)
    return pl.pallas_call(
        functools.partial(_flash_kernel, group=group, dq=dq, dv=dv, tk=tk),
        grid=(b, n_kv_heads, s // tq),
        in_specs=[pl.BlockSpec((None, tq, group * dq), lambda bi, h, qi: (bi, qi, q_col + h)),
                  pl.BlockSpec((None, sk, dq), lambda bi, h, qi: (bi, 0, k_col + h)),
                  pl.BlockSpec((None, sk, dv), lambda bi, h, qi: (bi, 0, v_col + h))],
        out_specs=pl.BlockSpec((None, tq, group * dv), lambda bi, h, qi: (bi, qi, h)),
        out_shape=jax.ShapeDtypeStruct((b, s, n_kv_heads * group * dv), BF16),
        compiler_params=_cparams(("parallel", "parallel", "parallel")),
        name="flash",
    )(q, k, v)


def _merge_kernel(oa_ref, ob_ref, om_ref, g0_ref, g1_ref, g2_ref, x_ref, wa_ref, wb_ref, wm_ref,
                  wo_ref, gn_ref, h_ref, n2_ref):
    merged = g0_ref[...].astype(F32) * _dot(oa_ref[...], wa_ref[...])
    merged = merged + g1_ref[...].astype(F32) * _dot(ob_ref[...], wb_ref[...])
    merged = merged + g2_ref[...].astype(F32) * _dot(om_ref[...], wm_ref[...])
    h = x_ref[...] + _dot(merged.astype(BF16), wo_ref[...])
    h_ref[...] = h
    n2_ref[...] = _rms(h, gn_ref[...]).astype(n2_ref.dtype)


def _merge(oa, ob, om, gates, x, wa, wb, wm, wo, gn, tm=256):
    t, d = x.shape
    const = lambda i: (0, 0)
    row = lambda a: pl.BlockSpec((tm, a.shape[1]), lambda i: (i, 0))
    wspec = lambda a: pl.BlockSpec(a.shape, const, pipeline_mode=pl.Buffered(1))
    gate = lambda br: pl.BlockSpec((tm, d), lambda i: (i, br))
    return pl.pallas_call(
        _merge_kernel,
        grid=(t // tm,),
        in_specs=[row(oa), row(ob), row(om), gate(0), gate(1), gate(2), row(x),
                  wspec(wa), wspec(wb), wspec(wm), wspec(wo), pl.BlockSpec((1, d), const)],
        out_specs=[pl.BlockSpec((tm, d), lambda i: (i, 0)), pl.BlockSpec((tm, d), lambda i: (i, 0))],
        out_shape=[jax.ShapeDtypeStruct((t, d), F32), jax.ShapeDtypeStruct((t, d), BF16)],
        compiler_params=_cparams(("parallel",)),
        name="merge",
    )(oa, ob, om, gates, gates, gates, x, wa, wb, wm, wo, gn)


def _mlp_kernel(n_ref, h_ref, wu_ref, wd_ref, g_ref, y_ref, acc_ref, *, final_norm):
    f = pl.program_id(1)
    u = _dot(n_ref[...], wu_ref[...])
    a = jnp.square(jnp.maximum(u, 0.0)).astype(BF16)
    part = _dot(a, wd_ref[...])

    @pl.when(f == 0)
    def _():
        acc_ref[...] = part

    @pl.when(f > 0)
    def _():
        acc_ref[...] += part

    @pl.when(f == pl.num_programs(1) - 1)
    def _():
        y = h_ref[...] + acc_ref[...]
        y_ref[...] = _rms(y, g_ref[...]) if final_norm else y


def _mlp(n2, h, wu, wd, g, final_norm, tm=512, tf=1024):
    t, d = h.shape
    dff = wu.shape[1]
    return pl.pallas_call(
        functools.partial(_mlp_kernel, final_norm=final_norm),
        grid=(t // tm, dff // tf),
        in_specs=[pl.BlockSpec((tm, d), lambda i, f: (i, 0)),
                  pl.BlockSpec((tm, d), lambda i, f: (i, 0)),
                  pl.BlockSpec((d, tf), lambda i, f: (0, f)),
                  pl.BlockSpec((tf, d), lambda i, f: (f, 0)),
                  pl.BlockSpec((1, d), lambda i, f: (0, 0))],
        out_specs=pl.BlockSpec((tm, d), lambda i, f: (i, 0)),
        out_shape=jax.ShapeDtypeStruct((t, d), F32),
        scratch_shapes=[pltpu.VMEM((tm, d), F32)],
        compiler_params=_cparams(("parallel", "arbitrary")),
        name="mlp",
    )(n2, h, wu, wd, g)


def _rope_tables(seq, dim):
    rows = jnp.repeat(jnp.arange(seq // GRID_W, dtype=jnp.int32), GRID_W)
    cols = jnp.tile(jnp.arange(GRID_W, dtype=jnp.int32), seq // GRID_W)
    h = dim // 2
    inv_freq = ROPE_THETA ** (-jnp.arange(0, h, 2, dtype=F32) / h)

    def cs(pos):
        ang = pos.astype(F32)[:, None] * inv_freq[None, :]
        return jnp.cos(ang), jnp.sin(ang)

    (cr, sr), (cc, sc) = cs(rows), cs(cols)
    z = jnp.zeros_like(sr)
    pad = jnp.zeros((seq, LANES - dim), F32)
    cos = jnp.concatenate([cr, cr, cc, cc, pad], axis=1)
    sin_dn = jnp.concatenate([-sr, z, -sc, z, pad], axis=1)
    sin_up = jnp.concatenate([z, sr, z, sc, pad], axis=1)
    return cos, sin_dn, sin_up


def _prep_layer(l, w_in, w_q_b, w_kv_b, w_mem_kv, w_br_a, w_br_b, w_br_m, w_out, w_up, w_down):
    d = w_in.shape[1]
    wi = w_in[l]
    a_w = HEADS_A * HEAD_DIM_A + 2 * KV_HEADS_A * HEAD_DIM_A
    lat_w = Q_LORA_B + KV_LORA_B + ROPE_DIM_B
    qm_w = HEADS_M * HEAD_DIM_M
    c_lat, c_qm, c_gate = a_w, a_w + lat_w, a_w + lat_w + qm_w
    w_a = wi[:, :c_lat].astype(BF16)
    w_lat = jnp.concatenate([wi[:, c_lat:c_qm], jnp.zeros((d, LANES - ROPE_DIM_B), F32)],
                            axis=1).astype(BF16)
    w_gq = jnp.concatenate([wi[:, c_gate:], wi[:, c_qm:c_gate]], axis=1).astype(BF16)
    wq = w_q_b[l].reshape(Q_LORA_B, HEADS_B, NOPE_DIM_B + ROPE_DIM_B)
    wq = jnp.pad(wq, ((0, 0), (0, 0), (0, QB_PAD - NOPE_DIM_B - ROPE_DIM_B)))
    wq = wq.reshape(Q_LORA_B, HEADS_B * QB_PAD).astype(BF16)
    wkv = w_kv_b[l].reshape(KV_LORA_B, HEADS_B, NOPE_DIM_B + V_DIM_B)
    wkv = jnp.concatenate([wkv[:, :, :NOPE_DIM_B].reshape(KV_LORA_B, -1),
                           wkv[:, :, NOPE_DIM_B:].reshape(KV_LORA_B, -1)], axis=1).astype(BF16)
    return dict(w_a=w_a, w_lat=w_lat, w_gq=w_gq, wq=wq, wkv=wkv,
                w_mem=w_mem_kv[l].astype(BF16), w_br_a=w_br_a[l].astype(BF16),
                w_br_b=w_br_b[l].astype(BF16), w_br_m=w_br_m[l].astype(BF16),
                w_out=w_out[l].astype(BF16), w_up=w_up[l].astype(BF16),
                w_down=w_down[l].astype(BF16))


def _trunk(x, mem, layers, gains, g_final):
    b, s, d = x.shape
    t = b * s
    n_mem = mem.shape[1]
    tabs_a = _rope_tables(s, HEAD_DIM_A)
    tabs_b = _rope_tables(s, ROPE_DIM_B)
    h = x.reshape(t, d)
    mem2 = mem.reshape(b * n_mem, d)
    depth = len(layers)
    for l, (w, g) in enumerate(zip(layers, gains)):
        n = _norm(h, g["g_mix"])
        qkv = _qkv_a(n, w["w_a"], g["g_qa"], g["g_ka"], tabs_a, s).reshape(b, s, -1)
        q_b, k_b, v_b = _latent(n, w["w_lat"], g["g_cq"], g["g_ckv"], w["wq"], w["wkv"], tabs_b, s)
        gq = _gates_qm(n, w["w_gq"], N_BRANCH * d)
        kv_m = _mem_kv(mem2, g["g_mem"], w["w_mem"]).reshape(b, n_mem, -1)

        o_a = _flash(qkv, qkv, qkv, n_kv_heads=KV_HEADS_A, group=HEADS_A // KV_HEADS_A,
                     dq=HEAD_DIM_A, dv=HEAD_DIM_A, q_col=0, k_col=HEADS_A,
                     v_col=HEADS_A + KV_HEADS_A, tq=256, tk=512)
        o_b = _flash(q_b.reshape(b, s, -1), k_b.reshape(b, s, -1), v_b.reshape(b, s, -1),
                     n_kv_heads=HEADS_B, group=1, dq=QB_PAD, dv=V_DIM_B, q_col=0, k_col=0, v_col=0,
                     tq=1024, tk=512)
        o_m = _flash(gq.reshape(b, s, -1), kv_m, kv_m, n_kv_heads=HEADS_M, group=1,
                     dq=HEAD_DIM_M, dv=HEAD_DIM_M, q_col=N_BRANCH * d // HEAD_DIM_M, k_col=0,
                     v_col=HEADS_M, tq=1024, tk=n_mem)

        h, n2 = _merge(o_a.reshape(t, -1), o_b.reshape(t, -1), o_m.reshape(t, -1), gq, h,
                       w["w_br_a"], w["w_br_b"], w["w_br_m"], w["w_out"], g["g_mlp"])
        last = l == depth - 1
        h = _mlp(n2, h, w["w_up"], w["w_down"], g_final if last else g["g_mlp"], final_norm=last)
    return h.reshape(b, s, d)


def kernel(x_prompt, x_sample, mem_prompt, mem_sample, g_mix, w_in, g_qa, g_ka, g_cq, w_q_b, g_ckv,
           w_kv_b, g_mem, w_mem_kv, w_br_a, w_br_b, w_br_m, w_out, g_mlp, w_up, w_down, g_final):
    depth = w_in.shape[0]
    layers = [_prep_layer(l, w_in, w_q_b, w_kv_b, w_mem_kv, w_br_a, w_br_b, w_br_m, w_out, w_up,
                          w_down) for l in range(depth)]
    gains = [dict(g_mix=g_mix[l][None], g_qa=g_qa[l][None], g_ka=g_ka[l][None], g_cq=g_cq[l][None],
                  g_ckv=g_ckv[l][None], g_mem=g_mem[l][None], g_mlp=g_mlp[l][None])
             for l in range(depth)]
    gf = g_final[None]
    y_prompt = _trunk(x_prompt, mem_prompt, layers, gains, gf)
    y_sample = _trunk(x_sample, mem_sample, layers, gains, gf)
    return (y_prompt, y_sample)
```

```python
import functools
import math

import jax
import jax.numpy as jnp
from jax import lax
from jax.experimental import pallas as pl
from jax.experimental.pallas import tpu as pltpu

F32 = jnp.float32
BF16 = jnp.bfloat16

EPS = 1e-6
ROPE_THETA = 10000.0
GRID_W = 64
LOG2E = math.log2(math.e)

HEADS_A, KV_HEADS_A, HEAD_DIM_A = 8, 2, 128
HEADS_B, Q_LORA_B, KV_LORA_B = 8, 512, 256
NOPE_DIM_B, ROPE_DIM_B, V_DIM_B = 128, 64, 128
HEADS_M, HEAD_DIM_M = 4, 128
N_BRANCH = 3
LANES = 128
QB_PAD = 256

VMEM_LIMIT = 56 * 1024 * 1024


def _cparams(sem):
    return pltpu.CompilerParams(dimension_semantics=sem, vmem_limit_bytes=VMEM_LIMIT)


def _rms(x, g):
    y = x * lax.rsqrt(jnp.mean(x * x, axis=-1, keepdims=True) + EPS)
    return y * g


def _dot(a, b):
    return jnp.dot(a, b, preferred_element_type=F32)


def _rope(x, cos, sin_dn, sin_up, half):
    return (x * cos + pltpu.roll(x, LANES - half, 1) * sin_dn
            + pltpu.roll(x, half, 1) * sin_up)


def _norm_kernel(x_ref, g_ref, o_ref):
    o_ref[...] = _rms(x_ref[...], g_ref[...]).astype(o_ref.dtype)


def _norm(x, g, tm=512):
    t, d = x.shape
    return pl.pallas_call(
        _norm_kernel,
        grid=(t // tm,),
        in_specs=[pl.BlockSpec((tm, d), lambda i: (i, 0)),
                  pl.BlockSpec((1, d), lambda i: (0, 0))],
        out_specs=pl.BlockSpec((tm, d), lambda i: (i, 0)),
        out_shape=jax.ShapeDtypeStruct((t, d), BF16),
        compiler_params=_cparams(("parallel",)),
        name="norm",
    )(x, g)


def _qkv_a_kernel(n_ref, w_ref, gq_ref, gk_ref, cos_ref, sdn_ref, sup_ref, qk_ref, vt_ref, *,
                  q_scale, n_q_blocks):
    j = pl.program_id(1)
    z = _dot(n_ref[...], w_ref[...])
    cos, sdn, sup = cos_ref[...], sdn_ref[...], sup_ref[...]
    n_heads = z.shape[1] // LANES

    def normed_rope(h, g):
        zh = z[:, h * LANES:(h + 1) * LANES]
        return _rope(_rms(zh, g), cos, sdn, sup, HEAD_DIM_A // 4)

    @pl.when(j < n_q_blocks)
    def _():
        for h in range(n_heads):
            qk_ref[:, h * LANES:(h + 1) * LANES] = (
                normed_rope(h, gq_ref[...]) * q_scale).astype(qk_ref.dtype)

    @pl.when(j == n_q_blocks)
    def _():
        for h in range(n_heads):
            qk_ref[:, h * LANES:(h + 1) * LANES] = normed_rope(h, gk_ref[...]).astype(qk_ref.dtype)

    @pl.when(j == n_q_blocks + 1)
    def _():
        vt_ref[...] = z.T.astype(vt_ref.dtype)


def _qkv_a(n, w, gq, gk, tabs, seq, tm=1024):
    t, d = n.shape
    tn = KV_HEADS_A * HEAD_DIM_A
    n_q_blocks = HEADS_A * HEAD_DIM_A // tn
    cos, sdn, sup = tabs
    tm = min(tm, seq)
    nblk = seq // tm
    tab_spec = pl.BlockSpec((tm, LANES), lambda i, j: (i % nblk, 0))
    vec_spec = pl.BlockSpec((1, LANES), lambda i, j: (0, 0))
    return pl.pallas_call(
        functools.partial(_qkv_a_kernel, q_scale=HEAD_DIM_A ** -0.5 * LOG2E, n_q_blocks=n_q_blocks),
        grid=(t // tm, n_q_blocks + 2),
        in_specs=[pl.BlockSpec((tm, d), lambda i, j: (i, 0)),
                  pl.BlockSpec((d, tn), lambda i, j: (0, j)),
                  vec_spec, vec_spec, tab_spec, tab_spec, tab_spec],
        out_specs=[pl.BlockSpec((tm, tn), lambda i, j: (i, jnp.minimum(j, n_q_blocks))),
                   pl.BlockSpec((tn, tm), lambda i, j: (0, i))],
        out_shape=[jax.ShapeDtypeStruct((t, (n_q_blocks + 1) * tn), BF16),
                   jax.ShapeDtypeStruct((tn, t), BF16)],
        compiler_params=_cparams(("parallel", "arbitrary")),
        name="qkv_a",
    )(n, w, gq, gk, cos, sdn, sup)


def _latent_kernel(n_ref, wl_ref, gcq_ref, gckv_ref, wq_ref, wkv_ref, cos_ref, sdn_ref, sup_ref,
                   q_ref, k_ref, vt_ref, *, q_scale):
    lat = _dot(n_ref[...], wl_ref[...])
    cos, sdn, sup = cos_ref[...], sdn_ref[...], sup_ref[...]
    half = ROPE_DIM_B // 4
    cq = _rms(lat[:, :Q_LORA_B], gcq_ref[...]).astype(BF16)
    qb = _dot(cq, wq_ref[...])
    for h in range(HEADS_B):
        c0 = h * QB_PAD
        q_ref[:, c0:c0 + LANES] = (qb[:, c0:c0 + LANES] * q_scale).astype(q_ref.dtype)
        q_ref[:, c0 + LANES:c0 + QB_PAD] = (
            _rope(qb[:, c0 + LANES:c0 + QB_PAD], cos, sdn, sup, half) * q_scale).astype(q_ref.dtype)
    ckv = _rms(lat[:, Q_LORA_B:Q_LORA_B + KV_LORA_B], gckv_ref[...]).astype(BF16)
    kvb = _dot(ckv, wkv_ref[...])
    kpe = _rope(lat[:, Q_LORA_B + KV_LORA_B:], cos, sdn, sup, half).astype(k_ref.dtype)
    for h in range(HEADS_B):
        c0 = h * QB_PAD
        k_ref[:, c0:c0 + LANES] = kvb[:, h * LANES:(h + 1) * LANES].astype(k_ref.dtype)
        k_ref[:, c0 + LANES:c0 + QB_PAD] = kpe
    vt_ref[...] = kvb[:, HEADS_B * NOPE_DIM_B:].T.astype(vt_ref.dtype)


def _latent(n, wl, gcq, gckv, wq, wkv, tabs, seq, tm=256):
    t, d = n.shape
    cos, sdn, sup = tabs
    nblk = seq // tm
    const = lambda i: (0, 0)
    tab_spec = pl.BlockSpec((tm, LANES), lambda i: (i % nblk, 0))
    row = lambda w: pl.BlockSpec((tm, w), lambda i: (i, 0))
    q_scale = (NOPE_DIM_B + ROPE_DIM_B) ** -0.5 * LOG2E
    return pl.pallas_call(
        functools.partial(_latent_kernel, q_scale=q_scale),
        grid=(t // tm,),
        in_specs=[row(d),
                  pl.BlockSpec(wl.shape, const), pl.BlockSpec(gcq.shape, const),
                  pl.BlockSpec(gckv.shape, const), pl.BlockSpec(wq.shape, const),
                  pl.BlockSpec(wkv.shape, const), tab_spec, tab_spec, tab_spec],
        out_specs=[row(HEADS_B * QB_PAD), row(HEADS_B * QB_PAD),
                   pl.BlockSpec((HEADS_B * V_DIM_B, tm), lambda i: (0, i))],
        out_shape=[jax.ShapeDtypeStruct((t, HEADS_B * QB_PAD), BF16),
                   jax.ShapeDtypeStruct((t, HEADS_B * QB_PAD), BF16),
                   jax.ShapeDtypeStruct((HEADS_B * V_DIM_B, t), BF16)],
        compiler_params=_cparams(("parallel",)),
        name="latent",
    )(n, wl, gcq, gckv, wq, wkv, cos, sdn, sup)


def _gate_kernel(n_ref, w_ref, o_ref, *, n_gate_blocks, q_scale):
    j = pl.program_id(1)
    z = _dot(n_ref[...], w_ref[...])

    @pl.when(j < n_gate_blocks)
    def _():
        o_ref[...] = jax.nn.sigmoid(z).astype(o_ref.dtype)

    @pl.when(j >= n_gate_blocks)
    def _():
        o_ref[...] = (z * q_scale).astype(o_ref.dtype)


def _gates_qm(n, w, n_gate_cols, tm=1024, tn=512):
    t, d = n.shape
    nn = w.shape[1]
    return pl.pallas_call(
        functools.partial(_gate_kernel, n_gate_blocks=n_gate_cols // tn,
                          q_scale=HEAD_DIM_M ** -0.5 * LOG2E),
        grid=(t // tm, nn // tn),
        in_specs=[pl.BlockSpec((tm, d), lambda i, j: (i, 0)),
                  pl.BlockSpec((d, tn), lambda i, j: (0, j))],
        out_specs=pl.BlockSpec((tm, tn), lambda i, j: (i, j)),
        out_shape=jax.ShapeDtypeStruct((t, nn), BF16),
        compiler_params=_cparams(("parallel", "arbitrary")),
        name="gates_qm",
    )(n, w)


def _mem_kv_kernel(m_ref, g_ref, w_ref, k_ref, vt_ref):
    nm = _rms(m_ref[...], g_ref[...]).astype(BF16)
    z = _dot(nm, w_ref[...])
    nk = k_ref.shape[1]
    k_ref[...] = z[:, :nk].astype(k_ref.dtype)
    vt_ref[...] = z[:, nk:].T.astype(vt_ref.dtype)


def _mem_kv(mem, g, w, tm=256):
    t, d = mem.shape
    nn = w.shape[1] // 2
    return pl.pallas_call(
        _mem_kv_kernel,
        grid=(t // tm,),
        in_specs=[pl.BlockSpec((tm, d), lambda i: (i, 0)),
                  pl.BlockSpec((1, d), lambda i: (0, 0)),
                  pl.BlockSpec((d, 2 * nn), lambda i: (0, 0))],
        out_specs=[pl.BlockSpec((tm, nn), lambda i: (i, 0)),
                   pl.BlockSpec((nn, tm), lambda i: (0, i))],
        out_shape=[jax.ShapeDtypeStruct((t, nn), BF16), jax.ShapeDtypeStruct((nn, t), BF16)],
        compiler_params=_cparams(("parallel",)),
        name="mem_kv",
    )(mem, g, w)


def _flash_kernel(q_ref, k_ref, vt_ref, o_ref, s_ref, *, group, dq, dv, tk):
    tq = q_ref.shape[0]
    nk = k_ref.shape[0] // tk
    if group > 1:
        q = jnp.concatenate([q_ref[:, g * dq:(g + 1) * dq] for g in range(group)], axis=0)
    else:
        q = q_ref[...]
    cols = group * tq

    def scores(j):
        ks = k_ref[pl.ds(pl.multiple_of(j * tk, tk), tk), :]
        s = lax.dot_general(ks, q, (((1,), (1,)), ((), ())), preferred_element_type=F32)
        return s, jnp.max(s, axis=0, keepdims=True)

    def consume(j, s, s_max, m, l, acc):
        vt = vt_ref[:, pl.ds(pl.multiple_of(j * tk, tk), tk)]
        m_new = jnp.maximum(m, s_max)
        alpha = jnp.exp2(m - m_new)
        p = jnp.exp2(s - m_new)
        l_new = alpha * l + jnp.sum(p, axis=0, keepdims=True)
        acc_new = alpha * acc + _dot(vt, p.astype(BF16))
        return m_new, l_new, acc_new

    state = (jnp.full((1, cols), -jnp.inf, F32), jnp.zeros((1, cols), F32),
             jnp.zeros((dv, cols), F32))
    s0, c0 = scores(0)
    if nk == 1:
        _, l, acc = consume(0, s0, c0, *state)
    else:
        assert nk % 2 == 0
        s_ref[0] = s0

        def pair(i, carry):
            m, l, acc, c_even = carry
            j = 2 * i
            s_odd, c_odd = scores(j + 1)
            s_ref[1] = s_odd
            m, l, acc = consume(j, s_ref[0], c_even, m, l, acc)
            s_even, c_next = scores(j + 2)
            s_ref[0] = s_even
            m, l, acc = consume(j + 1, s_ref[1], c_odd, m, l, acc)
            return m, l, acc, c_next

        m, l, acc, c_even = lax.fori_loop(0, nk // 2 - 1, pair, state + (c0,))
        s_odd, c_odd = scores(nk - 1)
        s_ref[1] = s_odd
        m, l, acc = consume(nk - 2, s_ref[0], c_even, m, l, acc)
        _, l, acc = consume(nk - 1, s_ref[1], c_odd, m, l, acc)
    o = acc / l
    for g in range(group):
        o_ref[:, g * dv:(g + 1) * dv] = o[:, g * tq:(g + 1) * tq].T.astype(o_ref.dtype)


def _flash(q, k, vt, *, n_kv_heads, group, dq, dv, q_col, k_col, v_row, tq, tk):
    b, s, _ = q.shape
    sk = k.shape[1]
    tk = min(tk, sk)
    return pl.pallas_call(
        functools.partial(_flash_kernel, group=group, dq=dq, dv=dv, tk=tk),
        grid=(b, n_kv_heads, s // tq),
        in_specs=[pl.BlockSpec((None, tq, group * dq), lambda bi, h, qi: (bi, qi, q_col + h)),
                  pl.BlockSpec((None, sk, dq), lambda bi, h, qi: (bi, 0, k_col + h)),
                  pl.BlockSpec((dv, sk), lambda bi, h, qi: (v_row + h, bi))],
        out_specs=pl.BlockSpec((None, tq, group * dv), lambda bi, h, qi: (bi, qi, h)),
        out_shape=jax.ShapeDtypeStruct((b, s, n_kv_heads * group * dv), BF16),
        scratch_shapes=[pltpu.VMEM((2, tk, group * tq), F32)],
        compiler_params=_cparams(("parallel", "parallel", "parallel")),
        name="flash",
    )(q, k, vt)


def _merge_kernel(oa_ref, ob_ref, om_ref, g0_ref, g1_ref, g2_ref, x_ref, wa_ref, wb_ref, wm_ref,
                  wo_ref, gn_ref, h_ref, n2_ref):
    merged = g0_ref[...].astype(F32) * _dot(oa_ref[...], wa_ref[...])
    merged = merged + g1_ref[...].astype(F32) * _dot(ob_ref[...], wb_ref[...])
    merged = merged + g2_ref[...].astype(F32) * _dot(om_ref[...], wm_ref[...])
    h = x_ref[...] + _dot(merged.astype(BF16), wo_ref[...])
    h_ref[...] = h
    n2_ref[...] = _rms(h, gn_ref[...]).astype(n2_ref.dtype)


def _merge(oa, ob, om, gates, x, wa, wb, wm, wo, gn, tm=256):
    t, d = x.shape
    const = lambda i: (0, 0)
    row = lambda a: pl.BlockSpec((tm, a.shape[1]), lambda i: (i, 0))
    wspec = lambda a: pl.BlockSpec(a.shape, const, pipeline_mode=pl.Buffered(1))
    gate = lambda br: pl.BlockSpec((tm, d), lambda i: (i, br))
    return pl.pallas_call(
        _merge_kernel,
        grid=(t // tm,),
        in_specs=[row(oa), row(ob), row(om), gate(0), gate(1), gate(2), row(x),
                  wspec(wa), wspec(wb), wspec(wm), wspec(wo), pl.BlockSpec((1, d), const)],
        out_specs=[pl.BlockSpec((tm, d), lambda i: (i, 0)), pl.BlockSpec((tm, d), lambda i: (i, 0))],
        out_shape=[jax.ShapeDtypeStruct((t, d), F32), jax.ShapeDtypeStruct((t, d), BF16)],
        compiler_params=_cparams(("parallel",)),
        name="merge",
    )(oa, ob, om, gates, gates, gates, x, wa, wb, wm, wo, gn)


def _mlp_kernel(n_ref, h_ref, wu_ref, wd_ref, g_ref, y_ref, acc_ref, *, final_norm):
    f = pl.program_id(1)
    u = _dot(n_ref[...], wu_ref[...])
    a = jnp.square(jnp.maximum(u, 0.0)).astype(BF16)
    part = _dot(a, wd_ref[...])

    @pl.when(f == 0)
    def _():
        acc_ref[...] = part

    @pl.when(f > 0)
    def _():
        acc_ref[...] += part

    @pl.when(f == pl.num_programs(1) - 1)
    def _():
        y = h_ref[...] + acc_ref[...]
        y_ref[...] = _rms(y, g_ref[...]) if final_norm else y


def _mlp(n2, h, wu, wd, g, final_norm, tm=512, tf=1024):
    t, d = h.shape
    dff = wu.shape[1]
    return pl.pallas_call(
        functools.partial(_mlp_kernel, final_norm=final_norm),
        grid=(t // tm, dff // tf),
        in_specs=[pl.BlockSpec((tm, d), lambda i, f: (i, 0)),
                  pl.BlockSpec((tm, d), lambda i, f: (i, 0)),
                  pl.BlockSpec((d, tf), lambda i, f: (0, f)),
                  pl.BlockSpec((tf, d), lambda i, f: (f, 0)),
                  pl.BlockSpec((1, d), lambda i, f: (0, 0))],
        out_specs=pl.BlockSpec((tm, d), lambda i, f: (i, 0)),
        out_shape=jax.ShapeDtypeStruct((t, d), F32),
        scratch_shapes=[pltpu.VMEM((tm, d), F32)],
        compiler_params=_cparams(("parallel", "arbitrary")),
        name="mlp",
    )(n2, h, wu, wd, g)


def _rope_tables(seq, dim):
    rows = jnp.repeat(jnp.arange(seq // GRID_W, dtype=jnp.int32), GRID_W)
    cols = jnp.tile(jnp.arange(GRID_W, dtype=jnp.int32), seq // GRID_W)
    h = dim // 2
    inv_freq = ROPE_THETA ** (-jnp.arange(0, h, 2, dtype=F32) / h)

    def cs(pos):
        ang = pos.astype(F32)[:, None] * inv_freq[None, :]
        return jnp.cos(ang), jnp.sin(ang)

    (cr, sr), (cc, sc) = cs(rows), cs(cols)
    z = jnp.zeros_like(sr)
    pad = jnp.zeros((seq, LANES - dim), F32)
    cos = jnp.concatenate([cr, cr, cc, cc, pad], axis=1)
    sin_dn = jnp.concatenate([-sr, z, -sc, z, pad], axis=1)
    sin_up = jnp.concatenate([z, sr, z, sc, pad], axis=1)
    return cos, sin_dn, sin_up


def _prep_layer(l, w_in, w_q_b, w_kv_b, w_mem_kv, w_br_a, w_br_b, w_br_m, w_out, w_up, w_down):
    d = w_in.shape[1]
    wi = w_in[l]
    a_w = HEADS_A * HEAD_DIM_A + 2 * KV_HEADS_A * HEAD_DIM_A
    lat_w = Q_LORA_B + KV_LORA_B + ROPE_DIM_B
    qm_w = HEADS_M * HEAD_DIM_M
    c_lat, c_qm, c_gate = a_w, a_w + lat_w, a_w + lat_w + qm_w
    w_a = wi[:, :c_lat].astype(BF16)
    w_lat = jnp.concatenate([wi[:, c_lat:c_qm], jnp.zeros((d, LANES - ROPE_DIM_B), F32)],
                            axis=1).astype(BF16)
    w_gq = jnp.concatenate([wi[:, c_gate:], wi[:, c_qm:c_gate]], axis=1).astype(BF16)
    wq = w_q_b[l].reshape(Q_LORA_B, HEADS_B, NOPE_DIM_B + ROPE_DIM_B)
    wq = jnp.pad(wq, ((0, 0), (0, 0), (0, QB_PAD - NOPE_DIM_B - ROPE_DIM_B)))
    wq = wq.reshape(Q_LORA_B, HEADS_B * QB_PAD).astype(BF16)
    wkv = w_kv_b[l].reshape(KV_LORA_B, HEADS_B, NOPE_DIM_B + V_DIM_B)
    wkv = jnp.concatenate([wkv[:, :, :NOPE_DIM_B].reshape(KV_LORA_B, -1),
                           wkv[:, :, NOPE_DIM_B:].reshape(KV_LORA_B, -1)], axis=1).astype(BF16)
    return dict(w_a=w_a, w_lat=w_lat, w_gq=w_gq, wq=wq, wkv=wkv,
                w_mem=w_mem_kv[l].astype(BF16), w_br_a=w_br_a[l].astype(BF16),
                w_br_b=w_br_b[l].astype(BF16), w_br_m=w_br_m[l].astype(BF16),
                w_out=w_out[l].astype(BF16), w_up=w_up[l].astype(BF16),
                w_down=w_down[l].astype(BF16))


def _trunk(x, mem, layers, gains, g_final):
    b, s, d = x.shape
    t = b * s
    n_mem = mem.shape[1]
    tabs_a = _rope_tables(s, HEAD_DIM_A)
    tabs_b = _rope_tables(s, ROPE_DIM_B)
    h = x.reshape(t, d)
    mem2 = mem.reshape(b * n_mem, d)
    depth = len(layers)
    for l, (w, g) in enumerate(zip(layers, gains)):
        n = _norm(h, g["g_mix"])
        qk_a, vt_a = _qkv_a(n, w["w_a"], g["g_qa"], g["g_ka"], tabs_a, s)
        qk_a = qk_a.reshape(b, s, -1)
        q_b, k_b, vt_b = _latent(n, w["w_lat"], g["g_cq"], g["g_ckv"], w["wq"], w["wkv"], tabs_b, s)
        gq = _gates_qm(n, w["w_gq"], N_BRANCH * d)
        k_m, vt_m = _mem_kv(mem2, g["g_mem"], w["w_mem"])

        o_a = _flash(qk_a, qk_a, vt_a, n_kv_heads=KV_HEADS_A, group=HEADS_A // KV_HEADS_A,
                     dq=HEAD_DIM_A, dv=HEAD_DIM_A, q_col=0, k_col=HEADS_A, v_row=0, tq=256, tk=512)
        o_b = _flash(q_b.reshape(b, s, -1), k_b.reshape(b, s, -1), vt_b,
                     n_kv_heads=HEADS_B, group=1, dq=QB_PAD, dv=V_DIM_B, q_col=0, k_col=0, v_row=0,
                     tq=1024, tk=512)
        o_m = _flash(gq.reshape(b, s, -1), k_m.reshape(b, n_mem, -1), vt_m, n_kv_heads=HEADS_M,
                     group=1, dq=HEAD_DIM_M, dv=HEAD_DIM_M, q_col=N_BRANCH * d // HEAD_DIM_M,
                     k_col=0, v_row=0, tq=1024, tk=n_mem)

        h, n2 = _merge(o_a.reshape(t, -1), o_b.reshape(t, -1), o_m.reshape(t, -1), gq, h,
                       w["w_br_a"], w["w_br_b"], w["w_br_m"], w["w_out"], g["g_mlp"])
        last = l == depth - 1
        h = _mlp(n2, h, w["w_up"], w["w_down"], g_final if last else g["g_mlp"], final_norm=last)
    return h.reshape(b, s, d)


def kernel(x_prompt, x_sample, mem_prompt, mem_sample, g_mix, w_in, g_qa, g_ka, g_cq, w_q_b, g_ckv,
           w_kv_b, g_mem, w_mem_kv, w_br_a, w_br_b, w_br_m, w_out, g_mlp, w_up, w_down, g_final):
    depth = w_in.shape[0]
    layers = [_prep_layer(l, w_in, w_q_b, w_kv_b, w_mem_kv, w_br_a, w_br_b, w_br_m, w_out, w_up,
                          w_down) for l in range(depth)]
    gains = [dict(g_mix=g_mix[l][None], g_qa=g_qa[l][None], g_ka=g_ka[l][None], g_cq=g_cq[l][None],
                  g_ckv=g_ckv[l][None], g_mem=g_mem[l][None], g_mlp=g_mlp[l][None])
             for l in range(depth)]
    gf = g_final[None]
    y_prompt = _trunk(x_prompt, mem_prompt, layers, gains, gf)
    y_sample = _trunk(x_sample, mem_sample, layers, gains, gf)
    return (y_prompt, y_sample)
```

```python
import functools
import math

import jax
import jax.numpy as jnp
from jax import lax
from jax.experimental import pallas as pl
from jax.experimental.pallas import tpu as pltpu

F32 = jnp.float32
BF16 = jnp.bfloat16

EPS = 1e-6
ROPE_THETA = 10000.0
GRID_W = 64
LOG2E = math.log2(math.e)

HEADS_A, KV_HEADS_A, HEAD_DIM_A = 8, 2, 128
HEADS_B, Q_LORA_B, KV_LORA_B = 8, 512, 256
NOPE_DIM_B, ROPE_DIM_B, V_DIM_B = 128, 64, 128
HEADS_M, HEAD_DIM_M = 4, 128
N_BRANCH = 3
LANES = 128
QB_PAD = 256

VMEM_LIMIT = 56 * 1024 * 1024


def _cparams(sem):
    return pltpu.CompilerParams(dimension_semantics=sem, vmem_limit_bytes=VMEM_LIMIT)


def _rms(x, g):
    y = x * lax.rsqrt(jnp.mean(x * x, axis=-1, keepdims=True) + EPS)
    return y * g


def _dot(a, b):
    return jnp.dot(a, b, preferred_element_type=F32)


def _rope(x, cos, sin):
    return x * cos + pltpu.roll(x, LANES // 2, 1) * sin


def _norm_kernel(x_ref, g_ref, o_ref):
    o_ref[...] = _rms(x_ref[...], g_ref[...]).astype(o_ref.dtype)


def _norm(x, g, tm=512):
    t, d = x.shape
    return pl.pallas_call(
        _norm_kernel,
        grid=(t // tm,),
        in_specs=[pl.BlockSpec((tm, d), lambda i: (i, 0)),
                  pl.BlockSpec((1, d), lambda i: (0, 0))],
        out_specs=pl.BlockSpec((tm, d), lambda i: (i, 0)),
        out_shape=jax.ShapeDtypeStruct((t, d), BF16),
        compiler_params=_cparams(("parallel",)),
        name="norm",
    )(x, g)


def _qkv_a_kernel(n_ref, w_ref, g_ref, cos_ref, sin_ref, qk_ref, vt_ref, qm_ref, *, qm_scale):
    z = _dot(n_ref[...], w_ref[...])
    cos, sin = cos_ref[...], sin_ref[...]
    n_qk = qk_ref.shape[1]
    n_v = vt_ref.shape[0]
    for h in range(n_qk // LANES):
        c = slice(h * LANES, (h + 1) * LANES)
        qk_ref[:, c] = _rope(_rms(z[:, c], g_ref[:, c]), cos, sin).astype(qk_ref.dtype)
    vt_ref[...] = z[:, n_qk:n_qk + n_v].T.astype(vt_ref.dtype)
    qm_ref[...] = (z[:, n_qk + n_v:] * qm_scale).astype(qm_ref.dtype)


def _qkv_a(n, w, g, tabs, seq, tm=512):
    t, d = n.shape
    n_qk = (HEADS_A + KV_HEADS_A) * HEAD_DIM_A
    n_v = KV_HEADS_A * HEAD_DIM_A
    n_qm = HEADS_M * HEAD_DIM_M
    cos, sin = tabs
    nblk = seq // tm
    const = lambda i: (0, 0)
    tab_spec = pl.BlockSpec((tm, LANES), lambda i: (i % nblk, 0))
    return pl.pallas_call(
        functools.partial(_qkv_a_kernel, qm_scale=HEAD_DIM_M ** -0.5 * LOG2E),
        grid=(t // tm,),
        in_specs=[pl.BlockSpec((tm, d), lambda i: (i, 0)),
                  pl.BlockSpec(w.shape, const, pipeline_mode=pl.Buffered(1)),
                  pl.BlockSpec(g.shape, const), tab_spec, tab_spec],
        out_specs=[pl.BlockSpec((tm, n_qk), lambda i: (i, 0)),
                   pl.BlockSpec((n_v, tm), lambda i: (0, i)),
                   pl.BlockSpec((tm, n_qm), lambda i: (i, 0))],
        out_shape=[jax.ShapeDtypeStruct((t, n_qk), BF16),
                   jax.ShapeDtypeStruct((n_v, t), BF16),
                   jax.ShapeDtypeStruct((t, n_qm), BF16)],
        compiler_params=_cparams(("parallel",)),
        name="qkv_a",
    )(n, w, g, cos, sin)


def _latent_kernel(n_ref, wl_ref, gcq_ref, gckv_ref, wq_ref, wkv_ref, cos_ref, sin_ref,
                   q_ref, k_ref, vt_ref, *, q_scale):
    lat = _dot(n_ref[...], wl_ref[...])
    cos, sin = cos_ref[...], sin_ref[...]
    cq = _rms(lat[:, :Q_LORA_B], gcq_ref[...]).astype(BF16)
    qb = _dot(cq, wq_ref[...])
    for h in range(HEADS_B):
        c0 = h * QB_PAD
        q_ref[:, c0:c0 + LANES] = (qb[:, c0:c0 + LANES] * q_scale).astype(q_ref.dtype)
        q_ref[:, c0 + LANES:c0 + QB_PAD] = (
            _rope(qb[:, c0 + LANES:c0 + QB_PAD], cos, sin) * q_scale).astype(q_ref.dtype)
    ckv = _rms(lat[:, Q_LORA_B:Q_LORA_B + KV_LORA_B], gckv_ref[...]).astype(BF16)
    kvb = _dot(ckv, wkv_ref[...])
    kpe = _rope(lat[:, Q_LORA_B + KV_LORA_B:], cos, sin).astype(k_ref.dtype)
    for h in range(HEADS_B):
        c0 = h * QB_PAD
        k_ref[:, c0:c0 + LANES] = kvb[:, h * LANES:(h + 1) * LANES].astype(k_ref.dtype)
        k_ref[:, c0 + LANES:c0 + QB_PAD] = kpe
    vt_ref[...] = kvb[:, HEADS_B * NOPE_DIM_B:].T.astype(vt_ref.dtype)


def _latent(n, wl, gcq, gckv, wq, wkv, tabs, seq, tm=256):
    t, d = n.shape
    cos, sin = tabs
    nblk = seq // tm
    const = lambda i: (0, 0)
    tab_spec = pl.BlockSpec((tm, LANES), lambda i: (i % nblk, 0))
    row = lambda w: pl.BlockSpec((tm, w), lambda i: (i, 0))
    q_scale = (NOPE_DIM_B + ROPE_DIM_B) ** -0.5 * LOG2E
    return pl.pallas_call(
        functools.partial(_latent_kernel, q_scale=q_scale),
        grid=(t // tm,),
        in_specs=[row(d),
                  pl.BlockSpec(wl.shape, const), pl.BlockSpec(gcq.shape, const),
                  pl.BlockSpec(gckv.shape, const), pl.BlockSpec(wq.shape, const),
                  pl.BlockSpec(wkv.shape, const), tab_spec, tab_spec],
        out_specs=[row(HEADS_B * QB_PAD), row(HEADS_B * QB_PAD),
                   pl.BlockSpec((HEADS_B * V_DIM_B, tm), lambda i: (0, i))],
        out_shape=[jax.ShapeDtypeStruct((t, HEADS_B * QB_PAD), BF16),
                   jax.ShapeDtypeStruct((t, HEADS_B * QB_PAD), BF16),
                   jax.ShapeDtypeStruct((HEADS_B * V_DIM_B, t), BF16)],
        compiler_params=_cparams(("parallel",)),
        name="latent",
    )(n, wl, gcq, gckv, wq, wkv, cos, sin)


def _gate_kernel(n_ref, w_ref, o_ref):
    o_ref[...] = jax.nn.sigmoid(_dot(n_ref[...], w_ref[...])).astype(o_ref.dtype)


def _gates(n, w, tm=512, tn=2048):
    t, d = n.shape
    nn = w.shape[1]
    return pl.pallas_call(
        _gate_kernel,
        grid=(nn // tn, t // tm),
        in_specs=[pl.BlockSpec((tm, d), lambda j, i: (i, 0)),
                  pl.BlockSpec((d, tn), lambda j, i: (0, j))],
        out_specs=pl.BlockSpec((tm, tn), lambda j, i: (i, j)),
        out_shape=jax.ShapeDtypeStruct((t, nn), BF16),
        compiler_params=_cparams(("arbitrary", "arbitrary")),
        name="gates",
    )(n, w)


def _mem_kv_kernel(m_ref, g_ref, w_ref, k_ref, vt_ref):
    nm = _rms(m_ref[...], g_ref[...]).astype(BF16)
    z = _dot(nm, w_ref[...])
    nk = k_ref.shape[1]
    k_ref[...] = z[:, :nk].astype(k_ref.dtype)
    vt_ref[...] = z[:, nk:].T.astype(vt_ref.dtype)


def _mem_kv(mem, g, w, tm=256):
    t, d = mem.shape
    nn = w.shape[1] // 2
    return pl.pallas_call(
        _mem_kv_kernel,
        grid=(t // tm,),
        in_specs=[pl.BlockSpec((tm, d), lambda i: (i, 0)),
                  pl.BlockSpec((1, d), lambda i: (0, 0)),
                  pl.BlockSpec((d, 2 * nn), lambda i: (0, 0))],
        out_specs=[pl.BlockSpec((tm, nn), lambda i: (i, 0)),
                   pl.BlockSpec((nn, tm), lambda i: (0, i))],
        out_shape=[jax.ShapeDtypeStruct((t, nn), BF16), jax.ShapeDtypeStruct((nn, t), BF16)],
        compiler_params=_cparams(("parallel",)),
        name="mem_kv",
    )(mem, g, w)


def _flash_kernel(q_ref, k_ref, vt_ref, o_ref, s_ref, *, group, dq, dv, tk):
    tq = q_ref.shape[0]
    nk = k_ref.shape[0] // tk
    if group > 1:
        q = jnp.concatenate([q_ref[:, g * dq:(g + 1) * dq] for g in range(group)], axis=0)
    else:
        q = q_ref[...]
    cols = group * tq

    def scores(j):
        ks = k_ref[pl.ds(pl.multiple_of(j * tk, tk), tk), :]
        s = lax.dot_general(ks, q, (((1,), (1,)), ((), ())), preferred_element_type=F32)
        return s, jnp.max(s, axis=0, keepdims=True)

    def consume(j, s, s_max, m, l, acc):
        vt = vt_ref[:, pl.ds(pl.multiple_of(j * tk, tk), tk)]
        m_new = jnp.maximum(m, s_max)
        alpha = jnp.exp2(m - m_new)
        p = jnp.exp2(s - m_new)
        l_new = alpha * l + jnp.sum(p, axis=0, keepdims=True)
        acc_new = alpha * acc + _dot(vt, p.astype(BF16))
        return m_new, l_new, acc_new

    state = (jnp.full((1, cols), -jnp.inf, F32), jnp.zeros((1, cols), F32),
             jnp.zeros((dv, cols), F32))
    s0, c0 = scores(0)
    if nk == 1:
        _, l, acc = consume(0, s0, c0, *state)
    else:
        assert nk % 2 == 0
        s_ref[0] = s0

        def pair(i, carry):
            m, l, acc, c_even = carry
            j = 2 * i
            s_odd, c_odd = scores(j + 1)
            s_ref[1] = s_odd
            m, l, acc = consume(j, s_ref[0], c_even, m, l, acc)
            s_even, c_next = scores(j + 2)
            s_ref[0] = s_even
            m, l, acc = consume(j + 1, s_ref[1], c_odd, m, l, acc)
            return m, l, acc, c_next

        m, l, acc, c_even = lax.fori_loop(0, nk // 2 - 1, pair, state + (c0,))
        s_odd, c_odd = scores(nk - 1)
        s_ref[1] = s_odd
        m, l, acc = consume(nk - 2, s_ref[0], c_even, m, l, acc)
        _, l, acc = consume(nk - 1, s_ref[1], c_odd, m, l, acc)
    o = acc / l
    for g in range(group):
        o_ref[:, g * dv:(g + 1) * dv] = o[:, g * tq:(g + 1) * tq].T.astype(o_ref.dtype)


def _flash(q, k, vt, *, n_kv_heads, group, dq, dv, q_col, k_col, v_row, tq, tk):
    b, s, _ = q.shape
    sk = k.shape[1]
    tk = min(tk, sk)
    return pl.pallas_call(
        functools.partial(_flash_kernel, group=group, dq=dq, dv=dv, tk=tk),
        grid=(b, n_kv_heads, s // tq),
        in_specs=[pl.BlockSpec((None, tq, group * dq), lambda bi, h, qi: (bi, qi, q_col + h)),
                  pl.BlockSpec((None, sk, dq), lambda bi, h, qi: (bi, 0, k_col + h)),
                  pl.BlockSpec((dv, sk), lambda bi, h, qi: (v_row + h, bi))],
        out_specs=pl.BlockSpec((None, tq, group * dv), lambda bi, h, qi: (bi, qi, h)),
        out_shape=jax.ShapeDtypeStruct((b, s, n_kv_heads * group * dv), BF16),
        scratch_shapes=[pltpu.VMEM((2, tk, group * tq), F32)],
        compiler_params=_cparams(("parallel", "parallel", "parallel")),
        name="flash",
    )(q, k, vt)


def _merge_kernel(oa_ref, ob_ref, om_ref, g0_ref, g1_ref, g2_ref, x_ref, wa_ref, wb_ref, wm_ref,
                  wo_ref, gn_ref, h_ref, n2_ref):
    merged = g0_ref[...].astype(F32) * _dot(oa_ref[...], wa_ref[...])
    merged = merged + g1_ref[...].astype(F32) * _dot(ob_ref[...], wb_ref[...])
    merged = merged + g2_ref[...].astype(F32) * _dot(om_ref[...], wm_ref[...])
    h = x_ref[...] + _dot(merged.astype(BF16), wo_ref[...])
    h_ref[...] = h
    n2_ref[...] = _rms(h, gn_ref[...]).astype(n2_ref.dtype)


def _merge(oa, ob, om, gates, x, wa, wb, wm, wo, gn, tm=256):
    t, d = x.shape
    const = lambda i: (0, 0)
    row = lambda a: pl.BlockSpec((tm, a.shape[1]), lambda i: (i, 0))
    wspec = lambda a: pl.BlockSpec(a.shape, const, pipeline_mode=pl.Buffered(1))
    gate = lambda br: pl.BlockSpec((tm, d), lambda i: (i, br))
    return pl.pallas_call(
        _merge_kernel,
        grid=(t // tm,),
        in_specs=[row(oa), row(ob), row(om), gate(0), gate(1), gate(2), row(x),
                  wspec(wa), wspec(wb), wspec(wm), wspec(wo), pl.BlockSpec((1, d), const)],
        out_specs=[pl.BlockSpec((tm, d), lambda i: (i, 0)), pl.BlockSpec((tm, d), lambda i: (i, 0))],
        out_shape=[jax.ShapeDtypeStruct((t, d), F32), jax.ShapeDtypeStruct((t, d), BF16)],
        compiler_params=_cparams(("parallel",)),
        name="merge",
    )(oa, ob, om, gates, gates, gates, x, wa, wb, wm, wo, gn)


def _mlp_kernel(n_ref, h_ref, wu_ref, wd_ref, g_ref, y_ref, acc_ref, *, final_norm):
    f = pl.program_id(1)

    @pl.when(f == 0)
    def _():
        acc_ref[...] = jnp.zeros_like(acc_ref)

    u = _dot(n_ref[...], wu_ref[...])
    a = jnp.square(jnp.maximum(u, 0.0)).astype(BF16)
    acc_ref[...] += _dot(a, wd_ref[...])

    @pl.when(f == pl.num_programs(1) - 1)
    def _():
        y = h_ref[...] + acc_ref[...]
        y_ref[...] = _rms(y, g_ref[...]) if final_norm else y


def _mlp(n2, h, wu, wd, g, final_norm, tm=512, tf=1024):
    t, d = h.shape
    dff = wu.shape[1]
    return pl.pallas_call(
        functools.partial(_mlp_kernel, final_norm=final_norm),
        grid=(t // tm, dff // tf),
        in_specs=[pl.BlockSpec((tm, d), lambda i, f: (i, 0)),
                  pl.BlockSpec((tm, d), lambda i, f: (i, 0)),
                  pl.BlockSpec((d, tf), lambda i, f: (0, f)),
                  pl.BlockSpec((tf, d), lambda i, f: (f, 0)),
                  pl.BlockSpec((1, d), lambda i, f: (0, 0))],
        out_specs=pl.BlockSpec((tm, d), lambda i, f: (i, 0)),
        out_shape=jax.ShapeDtypeStruct((t, d), F32),
        scratch_shapes=[pltpu.VMEM((tm, d), F32)],
        compiler_params=_cparams(("parallel", "arbitrary")),
        name="mlp",
    )(n2, h, wu, wd, g)


def _half_split_perm(w, dim):
    q = dim // 4
    r_lo, r_hi, c_lo, c_hi = (w[..., i * q:(i + 1) * q] for i in range(4))
    pad = jnp.zeros(w.shape[:-1] + (LANES // 2 - 2 * q,), w.dtype)
    return jnp.concatenate([r_lo, c_lo, pad, r_hi, c_hi, pad], axis=-1)


def _rope_tables(seq, dim):
    n_rows = seq // GRID_W
    h = dim // 2
    inv_freq = ROPE_THETA ** (-jnp.arange(0, h, 2, dtype=F32) / h)

    def cs(n_pos):
        ang = jnp.arange(n_pos, dtype=jnp.int32).astype(F32)[:, None] * inv_freq[None, :]
        return jnp.cos(ang), jnp.sin(ang)

    def per_token(tab, axis):
        tab = jnp.expand_dims(tab, 1 - axis)
        return jnp.broadcast_to(tab, (n_rows, GRID_W, tab.shape[-1])).reshape(seq, -1)

    (cr, sr), (cc, sc) = cs(n_rows), cs(GRID_W)
    cr, sr, cc, sc = per_token(cr, 0), per_token(sr, 0), per_token(cc, 1), per_token(sc, 1)
    pad = jnp.zeros((seq, LANES // 2 - dim // 2), F32)
    cos = jnp.concatenate([cr, cc, pad, cr, cc, pad], axis=1)
    sin = jnp.concatenate([-sr, -sc, pad, sr, sc, pad], axis=1)
    return cos, sin


def _prep_layer(l, w_in, g_qa, g_ka, w_q_b, w_kv_b, w_mem_kv, w_br_a, w_br_b, w_br_m, w_out, w_up,
                w_down):
    d = w_in.shape[1]
    wi = w_in[l]
    nq, nk = HEADS_A * HEAD_DIM_A, KV_HEADS_A * HEAD_DIM_A
    lat_w = Q_LORA_B + KV_LORA_B + ROPE_DIM_B
    qm_w = HEADS_M * HEAD_DIM_M
    c_lat = nq + 2 * nk
    c_qm, c_gate = c_lat + lat_w, c_lat + lat_w + qm_w
    c_rope = c_lat + Q_LORA_B + KV_LORA_B

    def perm_heads(w, n_heads):
        return _half_split_perm(w.reshape(d, n_heads, HEAD_DIM_A), HEAD_DIM_A).reshape(d, -1)

    w_a = jnp.concatenate([perm_heads(wi[:, :nq], HEADS_A), perm_heads(wi[:, nq:nq + nk], KV_HEADS_A),
                           wi[:, nq + nk:c_lat], wi[:, c_qm:c_gate]], axis=1).astype(BF16)
    g_a = jnp.concatenate(
        [jnp.tile(_half_split_perm(g_qa[l], HEAD_DIM_A) * (HEAD_DIM_A ** -0.5 * LOG2E), HEADS_A),
         jnp.tile(_half_split_perm(g_ka[l], HEAD_DIM_A), KV_HEADS_A)])[None]
    w_lat = jnp.concatenate([wi[:, c_lat:c_rope], _half_split_perm(wi[:, c_rope:c_qm], ROPE_DIM_B)],
                            axis=1).astype(BF16)
    w_gate = wi[:, c_gate:].astype(BF16)
    wq = w_q_b[l].reshape(Q_LORA_B, HEADS_B, NOPE_DIM_B + ROPE_DIM_B)
    wq = jnp.concatenate([wq[:, :, :NOPE_DIM_B], _half_split_perm(wq[:, :, NOPE_DIM_B:], ROPE_DIM_B)],
                         axis=-1).reshape(Q_LORA_B, HEADS_B * QB_PAD).astype(BF16)
    wkv = w_kv_b[l].reshape(KV_LORA_B, HEADS_B, NOPE_DIM_B + V_DIM_B)
    wkv = jnp.concatenate([wkv[:, :, :NOPE_DIM_B].reshape(KV_LORA_B, -1),
                           wkv[:, :, NOPE_DIM_B:].reshape(KV_LORA_B, -1)], axis=1).astype(BF16)
    return dict(w_a=w_a, g_a=g_a, w_lat=w_lat, w_gate=w_gate, wq=wq, wkv=wkv,
                w_mem=w_mem_kv[l].astype(BF16), w_br_a=w_br_a[l].astype(BF16),
                w_br_b=w_br_b[l].astype(BF16), w_br_m=w_br_m[l].astype(BF16),
                w_out=w_out[l].astype(BF16), w_up=w_up[l].astype(BF16),
                w_down=w_down[l].astype(BF16))


def _trunk(x, mem, layers, gains, g_final):
    b, s, d = x.shape
    t = b * s
    n_mem = mem.shape[1]
    tabs_a = _rope_tables(s, HEAD_DIM_A)
    tabs_b = _rope_tables(s, ROPE_DIM_B)
    h = x.reshape(t, d)
    mem2 = mem.reshape(b * n_mem, d)
    depth = len(layers)
    for l, (w, g) in enumerate(zip(layers, gains)):
        n = _norm(h, g["g_mix"])
        qk_a, vt_a, q_m = _qkv_a(n, w["w_a"], w["g_a"], tabs_a, s)
        qk_a = qk_a.reshape(b, s, -1)
        q_b, k_b, vt_b = _latent(n, w["w_lat"], g["g_cq"], g["g_ckv"], w["wq"], w["wkv"], tabs_b, s)
        gates = _gates(n, w["w_gate"])
        k_m, vt_m = _mem_kv(mem2, g["g_mem"], w["w_mem"])

        o_a = _flash(qk_a, qk_a, vt_a, n_kv_heads=KV_HEADS_A, group=HEADS_A // KV_HEADS_A,
                     dq=HEAD_DIM_A, dv=HEAD_DIM_A, q_col=0, k_col=HEADS_A, v_row=0, tq=256, tk=512)
        o_b = _flash(q_b.reshape(b, s, -1), k_b.reshape(b, s, -1), vt_b,
                     n_kv_heads=HEADS_B, group=1, dq=QB_PAD, dv=V_DIM_B, q_col=0, k_col=0, v_row=0,
                     tq=1024, tk=512)
        o_m = _flash(q_m.reshape(b, s, -1), k_m.reshape(b, n_mem, -1), vt_m, n_kv_heads=HEADS_M,
                     group=1, dq=HEAD_DIM_M, dv=HEAD_DIM_M, q_col=0, k_col=0, v_row=0, tq=1024,
                     tk=n_mem)

        h, n2 = _merge(o_a.reshape(t, -1), o_b.reshape(t, -1), o_m.reshape(t, -1), gates, h,
                       w["w_br_a"], w["w_br_b"], w["w_br_m"], w["w_out"], g["g_mlp"])
        last = l == depth - 1
        h = _mlp(n2, h, w["w_up"], w["w_down"], g_final if last else g["g_mlp"], final_norm=last)
    return h.reshape(b, s, d)


def kernel(x_prompt, x_sample, mem_prompt, mem_sample, g_mix, w_in, g_qa, g_ka, g_cq, w_q_b, g_ckv,
           w_kv_b, g_mem, w_mem_kv, w_br_a, w_br_b, w_br_m, w_out, g_mlp, w_up, w_down, g_final):
    depth = w_in.shape[0]
    layers = [_prep_layer(l, w_in, g_qa, g_ka, w_q_b, w_kv_b, w_mem_kv, w_br_a, w_br_b, w_br_m,
                          w_out, w_up, w_down) for l in range(depth)]
    gains = [dict(g_mix=g_mix[l][None], g_cq=g_cq[l][None], g_ckv=g_ckv[l][None],
                  g_mem=g_mem[l][None], g_mlp=g_mlp[l][None]) for l in range(depth)]
    gf = g_final[None]
    y_prompt = _trunk(x_prompt, mem_prompt, layers, gains, gf)
    y_sample = _trunk(x_sample, mem_sample, layers, gains, gf)
    return (y_prompt, y_sample)
```

```python
import functools
import math

import jax
import jax.numpy as jnp
from jax import lax
from jax.experimental import pallas as pl
from jax.experimental.pallas import tpu as pltpu

F32 = jnp.float32
BF16 = jnp.bfloat16

EPS = 1e-6
ROPE_THETA = 10000.0
GRID_W = 64
LOG2E = math.log2(math.e)

HEADS_A, KV_HEADS_A, HEAD_DIM_A = 8, 2, 128
HEADS_B, Q_LORA_B, KV_LORA_B = 8, 512, 256
NOPE_DIM_B, ROPE_DIM_B, V_DIM_B = 128, 64, 128
HEADS_M, HEAD_DIM_M = 4, 128
N_BRANCH = 3
LANES = 128
BF16_SUBLANES = 16
QB_PAD = 256

VMEM_LIMIT = 56 * 1024 * 1024


def _cparams(sem):
    return pltpu.CompilerParams(dimension_semantics=sem, vmem_limit_bytes=VMEM_LIMIT)


def _rms(x, g):
    y = x * lax.rsqrt(jnp.mean(x * x, axis=-1, keepdims=True) + EPS)
    return y * g


def _dot(a, b):
    return jnp.dot(a, b, preferred_element_type=F32)


def _rope(x, cos, sin):
    return x * cos + pltpu.roll(x, LANES // 2, 1) * sin


def _norm_kernel(x_ref, g_ref, o_ref):
    o_ref[...] = _rms(x_ref[...], g_ref[...]).astype(o_ref.dtype)


def _norm(x, g, tm=512):
    t, d = x.shape
    return pl.pallas_call(
        _norm_kernel,
        grid=(t // tm,),
        in_specs=[pl.BlockSpec((tm, d), lambda i: (i, 0)),
                  pl.BlockSpec((1, d), lambda i: (0, 0))],
        out_specs=pl.BlockSpec((tm, d), lambda i: (i, 0)),
        out_shape=jax.ShapeDtypeStruct((t, d), BF16),
        compiler_params=_cparams(("parallel",)),
        name="norm",
    )(x, g)


def _qkv_a_kernel(n_ref, w_ref, g_ref, cos_ref, sin_ref, qk_ref, vt_ref, qm_ref, *, qm_scale):
    z = _dot(n_ref[...], w_ref[...])
    cos, sin = cos_ref[...], sin_ref[...]
    n_qk = qk_ref.shape[1]
    n_v = vt_ref.shape[0]
    for h in range(n_qk // LANES):
        c = slice(h * LANES, (h + 1) * LANES)
        qk_ref[:, c] = _rope(_rms(z[:, c], g_ref[:, c]), cos, sin).astype(qk_ref.dtype)
    vt_ref[...] = z[:, n_qk:n_qk + n_v].T.astype(vt_ref.dtype)
    qm_ref[...] = (z[:, n_qk + n_v:] * qm_scale).astype(qm_ref.dtype)


def _qkv_a(n, w, g, tabs, seq, tm=512):
    t, d = n.shape
    n_qk = (HEADS_A + KV_HEADS_A) * HEAD_DIM_A
    n_v = KV_HEADS_A * HEAD_DIM_A
    n_qm = HEADS_M * HEAD_DIM_M
    cos, sin = tabs
    nblk = seq // tm
    const = lambda i: (0, 0)
    tab_spec = pl.BlockSpec((tm, LANES), lambda i: (i % nblk, 0))
    return pl.pallas_call(
        functools.partial(_qkv_a_kernel, qm_scale=HEAD_DIM_M ** -0.5 * LOG2E),
        grid=(t // tm,),
        in_specs=[pl.BlockSpec((tm, d), lambda i: (i, 0)),
                  pl.BlockSpec(w.shape, const, pipeline_mode=pl.Buffered(1)),
                  pl.BlockSpec(g.shape, const), tab_spec, tab_spec],
        out_specs=[pl.BlockSpec((tm, n_qk), lambda i: (i, 0)),
                   pl.BlockSpec((n_v, tm), lambda i: (0, i)),
                   pl.BlockSpec((tm, n_qm), lambda i: (i, 0))],
        out_shape=[jax.ShapeDtypeStruct((t, n_qk), BF16),
                   jax.ShapeDtypeStruct((n_v, t), BF16),
                   jax.ShapeDtypeStruct((t, n_qm), BF16)],
        compiler_params=_cparams(("parallel",)),
        name="qkv_a",
    )(n, w, g, cos, sin)


def _latent_kernel(n_ref, wl_ref, gcq_ref, gckv_ref, wq_ref, wkv_ref, cos_ref, sin_ref,
                   q_ref, k_ref, vt_ref, *, q_scale):
    lat = _dot(n_ref[...], wl_ref[...])
    cos, sin = cos_ref[...], sin_ref[...]
    cq = _rms(lat[:, :Q_LORA_B], gcq_ref[...]).astype(BF16)
    qb = _dot(cq, wq_ref[...])
    for h in range(HEADS_B):
        c0 = h * QB_PAD
        q_ref[:, c0:c0 + LANES] = (qb[:, c0:c0 + LANES] * q_scale).astype(q_ref.dtype)
        q_ref[:, c0 + LANES:c0 + QB_PAD] = (
            _rope(qb[:, c0 + LANES:c0 + QB_PAD], cos, sin) * q_scale).astype(q_ref.dtype)
    ckv = _rms(lat[:, Q_LORA_B:Q_LORA_B + KV_LORA_B], gckv_ref[...]).astype(BF16)
    kvb = _dot(ckv, wkv_ref[...])
    kpe = _rope(lat[:, Q_LORA_B + KV_LORA_B:], cos, sin).astype(k_ref.dtype)
    for h in range(HEADS_B):
        c0 = h * QB_PAD
        k_ref[:, c0:c0 + LANES] = kvb[:, h * LANES:(h + 1) * LANES].astype(k_ref.dtype)
        k_ref[:, c0 + LANES:c0 + QB_PAD] = kpe
    vt_ref[...] = kvb[:, HEADS_B * NOPE_DIM_B:].T.astype(vt_ref.dtype)


def _latent(n, wl, gcq, gckv, wq, wkv, tabs, seq, tm=256):
    t, d = n.shape
    cos, sin = tabs
    nblk = seq // tm
    const = lambda i: (0, 0)
    tab_spec = pl.BlockSpec((tm, LANES), lambda i: (i % nblk, 0))
    row = lambda w: pl.BlockSpec((tm, w), lambda i: (i, 0))
    q_scale = (NOPE_DIM_B + ROPE_DIM_B) ** -0.5 * LOG2E
    return pl.pallas_call(
        functools.partial(_latent_kernel, q_scale=q_scale),
        grid=(t // tm,),
        in_specs=[row(d),
                  pl.BlockSpec(wl.shape, const), pl.BlockSpec(gcq.shape, const),
                  pl.BlockSpec(gckv.shape, const), pl.BlockSpec(wq.shape, const),
                  pl.BlockSpec(wkv.shape, const), tab_spec, tab_spec],
        out_specs=[row(HEADS_B * QB_PAD), row(HEADS_B * QB_PAD),
                   pl.BlockSpec((HEADS_B * V_DIM_B, tm), lambda i: (0, i))],
        out_shape=[jax.ShapeDtypeStruct((t, HEADS_B * QB_PAD), BF16),
                   jax.ShapeDtypeStruct((t, HEADS_B * QB_PAD), BF16),
                   jax.ShapeDtypeStruct((HEADS_B * V_DIM_B, t), BF16)],
        compiler_params=_cparams(("parallel",)),
        name="latent",
    )(n, wl, gcq, gckv, wq, wkv, cos, sin)


def _gate_kernel(n_ref, w_ref, o_ref):
    o_ref[...] = jax.nn.sigmoid(_dot(n_ref[...], w_ref[...])).astype(o_ref.dtype)


def _gates(n, w, tm=512, tn=2048):
    t, d = n.shape
    nn = w.shape[1]
    return pl.pallas_call(
        _gate_kernel,
        grid=(nn // tn, t // tm),
        in_specs=[pl.BlockSpec((tm, d), lambda j, i: (i, 0)),
                  pl.BlockSpec((d, tn), lambda j, i: (0, j))],
        out_specs=pl.BlockSpec((tm, tn), lambda j, i: (i, j)),
        out_shape=jax.ShapeDtypeStruct((t, nn), BF16),
        compiler_params=_cparams(("arbitrary", "arbitrary")),
        name="gates",
    )(n, w)


def _mem_kv_kernel(m_ref, g_ref, w_ref, k_ref, vt_ref):
    nm = _rms(m_ref[...], g_ref[...]).astype(BF16)
    z = _dot(nm, w_ref[...])
    nk = k_ref.shape[1]
    k_ref[...] = z[:, :nk].astype(k_ref.dtype)
    vt_ref[...] = z[:, nk:].T.astype(vt_ref.dtype)


def _mem_kv(mem, g, w, tm=256):
    t, d = mem.shape
    nn = w.shape[1] // 2
    return pl.pallas_call(
        _mem_kv_kernel,
        grid=(t // tm,),
        in_specs=[pl.BlockSpec((tm, d), lambda i: (i, 0)),
                  pl.BlockSpec((1, d), lambda i: (0, 0)),
                  pl.BlockSpec((d, 2 * nn), lambda i: (0, 0))],
        out_specs=[pl.BlockSpec((tm, nn), lambda i: (i, 0)),
                   pl.BlockSpec((nn, tm), lambda i: (0, i))],
        out_shape=[jax.ShapeDtypeStruct((t, nn), BF16), jax.ShapeDtypeStruct((nn, t), BF16)],
        compiler_params=_cparams(("parallel",)),
        name="mem_kv",
    )(mem, g, w)


def _flash_kernel(q_ref, k_ref, vt_ref, o_ref, s_ref, *, group, dq, dv, tk):
    tq = q_ref.shape[0]
    nk = k_ref.shape[0] // tk
    if group > 1:
        q = jnp.concatenate([q_ref[:, g * dq:(g + 1) * dq] for g in range(group)], axis=0)
    else:
        q = q_ref[...]
    cols = group * tq

    ones = jnp.ones((BF16_SUBLANES, tk), BF16)

    def scores(j):
        ks = k_ref[pl.ds(pl.multiple_of(j * tk, tk), tk), :]
        s = lax.dot_general(ks, q, (((1,), (1,)), ((), ())), preferred_element_type=F32)
        return s, jnp.max(s, axis=0, keepdims=True)

    def consume(j, s, s_max, m, acc):
        vt = vt_ref[:, pl.ds(pl.multiple_of(j * tk, tk), tk)]
        m_new = jnp.maximum(m, s_max)
        alpha = jnp.exp2(m - m_new)
        p = jnp.exp2((s - m_new).astype(BF16))
        acc_new = alpha * acc + _dot(jnp.concatenate([vt, ones], axis=0), p)
        return m_new, acc_new

    state = (jnp.full((1, cols), -jnp.inf, F32), jnp.zeros((dv + BF16_SUBLANES, cols), F32))
    s0, c0 = scores(0)
    if nk == 1:
        _, acc = consume(0, s0, c0, *state)
    else:
        assert nk % 2 == 0
        s_ref[0] = s0

        def pair(i, carry):
            m, acc, c_even = carry
            j = 2 * i
            s_odd, c_odd = scores(j + 1)
            s_ref[1] = s_odd
            m, acc = consume(j, s_ref[0], c_even, m, acc)
            s_even, c_next = scores(j + 2)
            s_ref[0] = s_even
            m, acc = consume(j + 1, s_ref[1], c_odd, m, acc)
            return m, acc, c_next

        m, acc, c_even = lax.fori_loop(0, nk // 2 - 1, pair, state + (c0,))
        s_odd, c_odd = scores(nk - 1)
        s_ref[1] = s_odd
        m, acc = consume(nk - 2, s_ref[0], c_even, m, acc)
        _, acc = consume(nk - 1, s_ref[1], c_odd, m, acc)
    o = acc[:dv] / acc[dv:dv + 1]
    for g in range(group):
        o_ref[:, g * dv:(g + 1) * dv] = o[:, g * tq:(g + 1) * tq].T.astype(o_ref.dtype)


def _flash(q, k, vt, *, n_kv_heads, group, dq, dv, q_col, k_col, v_row, tq, tk):
    b, s, _ = q.shape
    sk = k.shape[1]
    tk = min(tk, sk)
    return pl.pallas_call(
        functools.partial(_flash_kernel, group=group, dq=dq, dv=dv, tk=tk),
        grid=(b, n_kv_heads, s // tq),
        in_specs=[pl.BlockSpec((None, tq, group * dq), lambda bi, h, qi: (bi, qi, q_col + h)),
                  pl.BlockSpec((None, sk, dq), lambda bi, h, qi: (bi, 0, k_col + h)),
                  pl.BlockSpec((dv, sk), lambda bi, h, qi: (v_row + h, bi))],
        out_specs=pl.BlockSpec((None, tq, group * dv), lambda bi, h, qi: (bi, qi, h)),
        out_shape=jax.ShapeDtypeStruct((b, s, n_kv_heads * group * dv), BF16),
        scratch_shapes=[pltpu.VMEM((2, tk, group * tq), F32)],
        compiler_params=_cparams(("parallel", "parallel", "parallel")),
        name="flash",
    )(q, k, vt)


def _merge_kernel(oa_ref, ob_ref, om_ref, g0_ref, g1_ref, g2_ref, x_ref, wa_ref, wb_ref, wm_ref,
                  wo_ref, gn_ref, h_ref, n2_ref):
    merged = g0_ref[...].astype(F32) * _dot(oa_ref[...], wa_ref[...])
    merged = merged + g1_ref[...].astype(F32) * _dot(ob_ref[...], wb_ref[...])
    merged = merged + g2_ref[...].astype(F32) * _dot(om_ref[...], wm_ref[...])
    h = x_ref[...] + _dot(merged.astype(BF16), wo_ref[...])
    h_ref[...] = h
    n2_ref[...] = _rms(h, gn_ref[...]).astype(n2_ref.dtype)


def _merge(oa, ob, om, gates, x, wa, wb, wm, wo, gn, tm=256):
    t, d = x.shape
    const = lambda i: (0, 0)
    row = lambda a: pl.BlockSpec((tm, a.shape[1]), lambda i: (i, 0))
    wspec = lambda a: pl.BlockSpec(a.shape, const, pipeline_mode=pl.Buffered(1))
    gate = lambda br: pl.BlockSpec((tm, d), lambda i: (i, br))
    return pl.pallas_call(
        _merge_kernel,
        grid=(t // tm,),
        in_specs=[row(oa), row(ob), row(om), gate(0), gate(1), gate(2), row(x),
                  wspec(wa), wspec(wb), wspec(wm), wspec(wo), pl.BlockSpec((1, d), const)],
        out_specs=[pl.BlockSpec((tm, d), lambda i: (i, 0)), pl.BlockSpec((tm, d), lambda i: (i, 0))],
        out_shape=[jax.ShapeDtypeStruct((t, d), F32), jax.ShapeDtypeStruct((t, d), BF16)],
        compiler_params=_cparams(("parallel",)),
        name="merge",
    )(oa, ob, om, gates, gates, gates, x, wa, wb, wm, wo, gn)


def _mlp_kernel(n_ref, h_ref, wu_ref, wd_ref, g_ref, y_ref, acc_ref, *, final_norm):
    f = pl.program_id(1)

    @pl.when(f == 0)
    def _():
        acc_ref[...] = jnp.zeros_like(acc_ref)

    u = _dot(n_ref[...], wu_ref[...])
    a = jnp.square(jnp.maximum(u, 0.0)).astype(BF16)
    acc_ref[...] += _dot(a, wd_ref[...])

    @pl.when(f == pl.num_programs(1) - 1)
    def _():
        y = h_ref[...] + acc_ref[...]
        y_ref[...] = _rms(y, g_ref[...]) if final_norm else y


def _mlp(n2, h, wu, wd, g, final_norm, tm=512, tf=1024):
    t, d = h.shape
    dff = wu.shape[1]
    return pl.pallas_call(
        functools.partial(_mlp_kernel, final_norm=final_norm),
        grid=(t // tm, dff // tf),
        in_specs=[pl.BlockSpec((tm, d), lambda i, f: (i, 0)),
                  pl.BlockSpec((tm, d), lambda i, f: (i, 0)),
                  pl.BlockSpec((d, tf), lambda i, f: (0, f)),
                  pl.BlockSpec((tf, d), lambda i, f: (f, 0)),
                  pl.BlockSpec((1, d), lambda i, f: (0, 0))],
        out_specs=pl.BlockSpec((tm, d), lambda i, f: (i, 0)),
        out_shape=jax.ShapeDtypeStruct((t, d), F32),
        scratch_shapes=[pltpu.VMEM((tm, d), F32)],
        compiler_params=_cparams(("parallel", "arbitrary")),
        name="mlp",
    )(n2, h, wu, wd, g)


def _half_split_perm(w, dim):
    q = dim // 4
    r_lo, r_hi, c_lo, c_hi = (w[..., i * q:(i + 1) * q] for i in range(4))
    pad = jnp.zeros(w.shape[:-1] + (LANES // 2 - 2 * q,), w.dtype)
    return jnp.concatenate([r_lo, c_lo, pad, r_hi, c_hi, pad], axis=-1)


def _rope_tables(seq, dim):
    n_rows = seq // GRID_W
    h = dim // 2
    inv_freq = ROPE_THETA ** (-jnp.arange(0, h, 2, dtype=F32) / h)

    def cs(n_pos):
        ang = jnp.arange(n_pos, dtype=jnp.int32).astype(F32)[:, None] * inv_freq[None, :]
        return jnp.cos(ang), jnp.sin(ang)

    def per_token(tab, axis):
        tab = jnp.expand_dims(tab, 1 - axis)
        return jnp.broadcast_to(tab, (n_rows, GRID_W, tab.shape[-1])).reshape(seq, -1)

    (cr, sr), (cc, sc) = cs(n_rows), cs(GRID_W)
    cr, sr, cc, sc = per_token(cr, 0), per_token(sr, 0), per_token(cc, 1), per_token(sc, 1)
    pad = jnp.zeros((seq, LANES // 2 - dim // 2), F32)
    cos = jnp.concatenate([cr, cc, pad, cr, cc, pad], axis=1)
    sin = jnp.concatenate([-sr, -sc, pad, sr, sc, pad], axis=1)
    return cos, sin


def _prep_layer(l, w_in, g_qa, g_ka, w_q_b, w_kv_b, w_mem_kv, w_br_a, w_br_b, w_br_m, w_out, w_up,
                w_down):
    d = w_in.shape[1]
    wi = w_in[l]
    nq, nk = HEADS_A * HEAD_DIM_A, KV_HEADS_A * HEAD_DIM_A
    lat_w = Q_LORA_B + KV_LORA_B + ROPE_DIM_B
    qm_w = HEADS_M * HEAD_DIM_M
    c_lat = nq + 2 * nk
    c_qm, c_gate = c_lat + lat_w, c_lat + lat_w + qm_w
    c_rope = c_lat + Q_LORA_B + KV_LORA_B

    def perm_heads(w, n_heads):
        return _half_split_perm(w.reshape(d, n_heads, HEAD_DIM_A), HEAD_DIM_A).reshape(d, -1)

    w_a = jnp.concatenate([perm_heads(wi[:, :nq], HEADS_A), perm_heads(wi[:, nq:nq + nk], KV_HEADS_A),
                           wi[:, nq + nk:c_lat], wi[:, c_qm:c_gate]], axis=1).astype(BF16)
    g_a = jnp.concatenate(
        [jnp.tile(_half_split_perm(g_qa[l], HEAD_DIM_A) * (HEAD_DIM_A ** -0.5 * LOG2E), HEADS_A),
         jnp.tile(_half_split_perm(g_ka[l], HEAD_DIM_A), KV_HEADS_A)])[None]
    w_lat = jnp.concatenate([wi[:, c_lat:c_rope], _half_split_perm(wi[:, c_rope:c_qm], ROPE_DIM_B)],
                            axis=1).astype(BF16)
    w_gate = wi[:, c_gate:].astype(BF16)
    wq = w_q_b[l].reshape(Q_LORA_B, HEADS_B, NOPE_DIM_B + ROPE_DIM_B)
    wq = jnp.concatenate([wq[:, :, :NOPE_DIM_B], _half_split_perm(wq[:, :, NOPE_DIM_B:], ROPE_DIM_B)],
                         axis=-1).reshape(Q_LORA_B, HEADS_B * QB_PAD).astype(BF16)
    wkv = w_kv_b[l].reshape(KV_LORA_B, HEADS_B, NOPE_DIM_B + V_DIM_B)
    wkv = jnp.concatenate([wkv[:, :, :NOPE_DIM_B].reshape(KV_LORA_B, -1),
                           wkv[:, :, NOPE_DIM_B:].reshape(KV_LORA_B, -1)], axis=1).astype(BF16)
    return dict(w_a=w_a, g_a=g_a, w_lat=w_lat, w_gate=w_gate, wq=wq, wkv=wkv,
                w_mem=w_mem_kv[l].astype(BF16), w_br_a=w_br_a[l].astype(BF16),
                w_br_b=w_br_b[l].astype(BF16), w_br_m=w_br_m[l].astype(BF16),
                w_out=w_out[l].astype(BF16), w_up=w_up[l].astype(BF16),
                w_down=w_down[l].astype(BF16))


def _trunk(x, mem, layers, gains, g_final):
    b, s, d = x.shape
    t = b * s
    n_mem = mem.shape[1]
    tabs_a = _rope_tables(s, HEAD_DIM_A)
    tabs_b = _rope_tables(s, ROPE_DIM_B)
    h = x.reshape(t, d)
    mem2 = mem.reshape(b * n_mem, d)
    depth = len(layers)
    for l, (w, g) in enumerate(zip(layers, gains)):
        n = _norm(h, g["g_mix"])
        qk_a, vt_a, q_m = _qkv_a(n, w["w_a"], w["g_a"], tabs_a, s)
        qk_a = qk_a.reshape(b, s, -1)
        q_b, k_b, vt_b = _latent(n, w["w_lat"], g["g_cq"], g["g_ckv"], w["wq"], w["wkv"], tabs_b, s)
        gates = _gates(n, w["w_gate"])
        k_m, vt_m = _mem_kv(mem2, g["g_mem"], w["w_mem"])

        o_a = _flash(qk_a, qk_a, vt_a, n_kv_heads=KV_HEADS_A, group=HEADS_A // KV_HEADS_A,
                     dq=HEAD_DIM_A, dv=HEAD_DIM_A, q_col=0, k_col=HEADS_A, v_row=0, tq=256, tk=512)
        o_b = _flash(q_b.reshape(b, s, -1), k_b.reshape(b, s, -1), vt_b,
                     n_kv_heads=HEADS_B, group=1, dq=QB_PAD, dv=V_DIM_B, q_col=0, k_col=0, v_row=0,
                     tq=1024, tk=512)
        o_m = _flash(q_m.reshape(b, s, -1), k_m.reshape(b, n_mem, -1), vt_m, n_kv_heads=HEADS_M,
                     group=1, dq=HEAD_DIM_M, dv=HEAD_DIM_M, q_col=0, k_col=0, v_row=0, tq=1024,
                     tk=n_mem)

        h, n2 = _merge(o_a.reshape(t, -1), o_b.reshape(t, -1), o_m.reshape(t, -1), gates, h,
                       w["w_br_a"], w["w_br_b"], w["w_br_m"], w["w_out"], g["g_mlp"])
        last = l == depth - 1
        h = _mlp(n2, h, w["w_up"], w["w_down"], g_final if last else g["g_mlp"], final_norm=last)
    return h.reshape(b, s, d)


def kernel(x_prompt, x_sample, mem_prompt, mem_sample, g_mix, w_in, g_qa, g_ka, g_cq, w_q_b, g_ckv,
           w_kv_b, g_mem, w_mem_kv, w_br_a, w_br_b, w_br_m, w_out, g_mlp, w_up, w_down, g_final):
    depth = w_in.shape[0]
    layers = [_prep_layer(l, w_in, g_qa, g_ka, w_q_b, w_kv_b, w_mem_kv, w_br_a, w_br_b, w_br_m,
                          w_out, w_up, w_down) for l in range(depth)]
    gains = [dict(g_mix=g_mix[l][None], g_cq=g_cq[l][None], g_ckv=g_ckv[l][None],
                  g_mem=g_mem[l][None], g_mlp=g_mlp[l][None]) for l in range(depth)]
    gf = g_final[None]
    y_prompt = _trunk(x_prompt, mem_prompt, layers, gains, gf)
    y_sample = _trunk(x_sample, mem_sample, layers, gains, gf)
    return (y_prompt, y_sample)
```

```python
import functools
import math

import jax
import jax.numpy as jnp
from jax import lax
from jax.experimental import pallas as pl
from jax.experimental.pallas import tpu as pltpu

F32 = jnp.float32
BF16 = jnp.bfloat16

EPS = 1e-6
ROPE_THETA = 10000.0
GRID_W = 64
LOG2E = math.log2(math.e)

HEADS_A, KV_HEADS_A, HEAD_DIM_A = 8, 2, 128
HEADS_B, Q_LORA_B, KV_LORA_B = 8, 512, 256
NOPE_DIM_B, ROPE_DIM_B, V_DIM_B = 128, 64, 128
HEADS_M, HEAD_DIM_M = 4, 128
N_BRANCH = 3
LANES = 128
QB_PAD = 256

VMEM_LIMIT = 56 * 1024 * 1024


def _cparams(sem):
    return pltpu.CompilerParams(dimension_semantics=sem, vmem_limit_bytes=VMEM_LIMIT)


def _rms(x, g):
    y = x * lax.rsqrt(jnp.mean(x * x, axis=-1, keepdims=True) + EPS)
    return y * g


def _dot(a, b):
    return jnp.dot(a, b, preferred_element_type=F32)


def _rope(x, cos, sin):
    return x * cos + pltpu.roll(x, LANES // 2, 1) * sin


def _norm_kernel(x_ref, g_ref, o_ref):
    o_ref[...] = _rms(x_ref[...], g_ref[...]).astype(o_ref.dtype)


def _norm(x, g, tm=512):
    t, d = x.shape
    return pl.pallas_call(
        _norm_kernel,
        grid=(t // tm,),
        in_specs=[pl.BlockSpec((tm, d), lambda i: (i, 0)),
                  pl.BlockSpec((1, d), lambda i: (0, 0))],
        out_specs=pl.BlockSpec((tm, d), lambda i: (i, 0)),
        out_shape=jax.ShapeDtypeStruct((t, d), BF16),
        compiler_params=_cparams(("parallel",)),
        name="norm",
    )(x, g)


def _qkv_a_kernel(n_ref, w_ref, g_ref, cos_ref, sin_ref, qk_ref, vt_ref, qm_ref, *, qm_scale):
    z = _dot(n_ref[...], w_ref[...])
    cos, sin = cos_ref[...], sin_ref[...]
    n_qk = qk_ref.shape[1]
    n_v = vt_ref.shape[0]
    for h in range(n_qk // LANES):
        c = slice(h * LANES, (h + 1) * LANES)
        qk_ref[:, c] = _rope(_rms(z[:, c], g_ref[:, c]), cos, sin).astype(qk_ref.dtype)
    vt_ref[...] = z[:, n_qk:n_qk + n_v].T.astype(vt_ref.dtype)
    qm_ref[...] = (z[:, n_qk + n_v:] * qm_scale).astype(qm_ref.dtype)


def _qkv_a(n, w, g, tabs, seq, tm=512):
    t, d = n.shape
    n_qk = (HEADS_A + KV_HEADS_A) * HEAD_DIM_A
    n_v = KV_HEADS_A * HEAD_DIM_A
    n_qm = HEADS_M * HEAD_DIM_M
    cos, sin = tabs
    nblk = seq // tm
    const = lambda i: (0, 0)
    tab_spec = pl.BlockSpec((tm, LANES), lambda i: (i % nblk, 0))
    return pl.pallas_call(
        functools.partial(_qkv_a_kernel, qm_scale=HEAD_DIM_M ** -0.5 * LOG2E),
        grid=(t // tm,),
        in_specs=[pl.BlockSpec((tm, d), lambda i: (i, 0)),
                  pl.BlockSpec(w.shape, const, pipeline_mode=pl.Buffered(1)),
                  pl.BlockSpec(g.shape, const), tab_spec, tab_spec],
        out_specs=[pl.BlockSpec((tm, n_qk), lambda i: (i, 0)),
                   pl.BlockSpec((n_v, tm), lambda i: (0, i)),
                   pl.BlockSpec((tm, n_qm), lambda i: (i, 0))],
        out_shape=[jax.ShapeDtypeStruct((t, n_qk), BF16),
                   jax.ShapeDtypeStruct((n_v, t), BF16),
                   jax.ShapeDtypeStruct((t, n_qm), BF16)],
        compiler_params=_cparams(("parallel",)),
        name="qkv_a",
    )(n, w, g, cos, sin)


def _latent_kernel(n_ref, wl_ref, gcq_ref, gckv_ref, wq_ref, wkv_ref, cos_ref, sin_ref,
                   q_ref, k_ref, vt_ref, *, q_scale):
    lat = _dot(n_ref[...], wl_ref[...])
    cos, sin = cos_ref[...], sin_ref[...]
    cq = _rms(lat[:, :Q_LORA_B], gcq_ref[...]).astype(BF16)
    qb = _dot(cq, wq_ref[...])
    for h in range(HEADS_B):
        c0 = h * QB_PAD
        q_ref[:, c0:c0 + LANES] = (qb[:, c0:c0 + LANES] * q_scale).astype(q_ref.dtype)
        q_ref[:, c0 + LANES:c0 + QB_PAD] = (
            _rope(qb[:, c0 + LANES:c0 + QB_PAD], cos, sin) * q_scale).astype(q_ref.dtype)
    ckv = _rms(lat[:, Q_LORA_B:Q_LORA_B + KV_LORA_B], gckv_ref[...]).astype(BF16)
    kvb = _dot(ckv, wkv_ref[...])
    kpe = _rope(lat[:, Q_LORA_B + KV_LORA_B:], cos, sin).astype(k_ref.dtype)
    for h in range(HEADS_B):
        c0 = h * QB_PAD
        k_ref[:, c0:c0 + LANES] = kvb[:, h * LANES:(h + 1) * LANES].astype(k_ref.dtype)
        k_ref[:, c0 + LANES:c0 + QB_PAD] = kpe
    vt_ref[...] = kvb[:, HEADS_B * NOPE_DIM_B:].T.astype(vt_ref.dtype)


def _latent(n, wl, gcq, gckv, wq, wkv, tabs, seq, tm=256):
    t, d = n.shape
    cos, sin = tabs
    nblk = seq // tm
    const = lambda i: (0, 0)
    tab_spec = pl.BlockSpec((tm, LANES), lambda i: (i % nblk, 0))
    row = lambda w: pl.BlockSpec((tm, w), lambda i: (i, 0))
    q_scale = (NOPE_DIM_B + ROPE_DIM_B) ** -0.5 * LOG2E
    return pl.pallas_call(
        functools.partial(_latent_kernel, q_scale=q_scale),
        grid=(t // tm,),
        in_specs=[row(d),
                  pl.BlockSpec(wl.shape, const), pl.BlockSpec(gcq.shape, const),
                  pl.BlockSpec(gckv.shape, const), pl.BlockSpec(wq.shape, const),
                  pl.BlockSpec(wkv.shape, const), tab_spec, tab_spec],
        out_specs=[row(HEADS_B * QB_PAD), row(HEADS_B * QB_PAD),
                   pl.BlockSpec((HEADS_B * V_DIM_B, tm), lambda i: (0, i))],
        out_shape=[jax.ShapeDtypeStruct((t, HEADS_B * QB_PAD), BF16),
                   jax.ShapeDtypeStruct((t, HEADS_B * QB_PAD), BF16),
                   jax.ShapeDtypeStruct((HEADS_B * V_DIM_B, t), BF16)],
        compiler_params=_cparams(("parallel",)),
        name="latent",
    )(n, wl, gcq, gckv, wq, wkv, cos, sin)


def _gate_kernel(n_ref, w_ref, o_ref):
    o_ref[...] = jax.nn.sigmoid(_dot(n_ref[...], w_ref[...])).astype(o_ref.dtype)


def _gates(n, w, tm=512, tn=2048):
    t, d = n.shape
    nn = w.shape[1]
    return pl.pallas_call(
        _gate_kernel,
        grid=(nn // tn, t // tm),
        in_specs=[pl.BlockSpec((tm, d), lambda j, i: (i, 0)),
                  pl.BlockSpec((d, tn), lambda j, i: (0, j))],
        out_specs=pl.BlockSpec((tm, tn), lambda j, i: (i, j)),
        out_shape=jax.ShapeDtypeStruct((t, nn), BF16),
        compiler_params=_cparams(("arbitrary", "arbitrary")),
        name="gates",
    )(n, w)


def _mem_kv_kernel(m_ref, g_ref, w_ref, k_ref, vt_ref):
    nm = _rms(m_ref[...], g_ref[...]).astype(BF16)
    z = _dot(nm, w_ref[...])
    nk = k_ref.shape[1]
    k_ref[...] = z[:, :nk].astype(k_ref.dtype)
    vt_ref[...] = z[:, nk:].T.astype(vt_ref.dtype)


def _mem_kv(mem, g, w, tm=256):
    t, d = mem.shape
    nn = w.shape[1] // 2
    return pl.pallas_call(
        _mem_kv_kernel,
        grid=(t // tm,),
        in_specs=[pl.BlockSpec((tm, d), lambda i: (i, 0)),
                  pl.BlockSpec((1, d), lambda i: (0, 0)),
                  pl.BlockSpec((d, 2 * nn), lambda i: (0, 0))],
        out_specs=[pl.BlockSpec((tm, nn), lambda i: (i, 0)),
                   pl.BlockSpec((nn, tm), lambda i: (0, i))],
        out_shape=[jax.ShapeDtypeStruct((t, nn), BF16), jax.ShapeDtypeStruct((nn, t), BF16)],
        compiler_params=_cparams(("parallel",)),
        name="mem_kv",
    )(mem, g, w)


def _flash_kernel(q_ref, qn_ref, k_ref, vt_ref, o_ref, s_ref, c_ref, *, group, dq, dv, tk):
    tq = q_ref.shape[0]
    nk = k_ref.shape[0] // tk
    cols = group * tq

    def stacked(ref):
        if group == 1:
            return ref[...]
        return jnp.concatenate([ref[:, g * dq:(g + 1) * dq] for g in range(group)], axis=0)

    q = stacked(q_ref)

    def scores(qs, j):
        ks = k_ref[pl.ds(pl.multiple_of(j * tk, tk), tk), :]
        s = lax.dot_general(ks, qs, (((1,), (1,)), ((), ())), preferred_element_type=F32)
        return s, jnp.max(s, axis=0, keepdims=True)

    def consume(j, s, s_max, m, l, acc):
        vt = vt_ref[:, pl.ds(pl.multiple_of(j * tk, tk), tk)]
        m_new = jnp.maximum(m, s_max)
        alpha = jnp.exp2(m - m_new)
        p = jnp.exp2(s - m_new)
        l_new = alpha * l + jnp.sum(p, axis=0, keepdims=True)
        acc_new = alpha * acc + _dot(vt, p.astype(BF16))
        return m_new, l_new, acc_new

    state = (jnp.full((1, cols), -jnp.inf, F32), jnp.zeros((1, cols), F32),
             jnp.zeros((dv, cols), F32))
    if nk == 1:
        _, l, acc = consume(0, *scores(q, 0), *state)
    else:
        assert nk % 2 == 0

        @pl.when(pl.program_id(2) == 0)
        def _():
            s0, c0 = scores(q, 0)
            s_ref[0] = s0
            c_ref[...] = c0

        def pair(i, carry):
            m, l, acc, c_even = carry
            j = 2 * i
            s_odd, c_odd = scores(q, j + 1)
            s_ref[1] = s_odd
            m, l, acc = consume(j, s_ref[0], c_even, m, l, acc)
            s_even, c_next = scores(q, j + 2)
            s_ref[0] = s_even
            m, l, acc = consume(j + 1, s_ref[1], c_odd, m, l, acc)
            return m, l, acc, c_next

        m, l, acc, c_even = lax.fori_loop(0, nk // 2 - 1, pair, state + (c_ref[...],))
        s_odd, c_odd = scores(q, nk - 1)
        s_ref[1] = s_odd
        m, l, acc = consume(nk - 2, s_ref[0], c_even, m, l, acc)
        s_next, c_next = scores(stacked(qn_ref), 0)
        s_ref[0] = s_next
        c_ref[...] = c_next
        _, l, acc = consume(nk - 1, s_ref[1], c_odd, m, l, acc)
    o = acc / l
    for g in range(group):
        o_ref[:, g * dv:(g + 1) * dv] = o[:, g * tq:(g + 1) * tq].T.astype(o_ref.dtype)


def _flash(q, k, vt, *, n_kv_heads, group, dq, dv, q_col, k_col, v_row, tq, tk):
    b, s, _ = q.shape
    sk = k.shape[1]
    tk = min(tk, sk)
    nq = s // tq
    q_block = (None, tq, group * dq)
    return pl.pallas_call(
        functools.partial(_flash_kernel, group=group, dq=dq, dv=dv, tk=tk),
        grid=(b, n_kv_heads, nq),
        in_specs=[pl.BlockSpec(q_block, lambda bi, h, qi: (bi, qi, q_col + h)),
                  pl.BlockSpec(q_block, lambda bi, h, qi: (bi, jnp.minimum(qi + 1, nq - 1), q_col + h)),
                  pl.BlockSpec((None, sk, dq), lambda bi, h, qi: (bi, 0, k_col + h)),
                  pl.BlockSpec((dv, sk), lambda bi, h, qi: (v_row + h, bi))],
        out_specs=pl.BlockSpec((None, tq, group * dv), lambda bi, h, qi: (bi, qi, h)),
        out_shape=jax.ShapeDtypeStruct((b, s, n_kv_heads * group * dv), BF16),
        scratch_shapes=[pltpu.VMEM((2, tk, group * tq), F32), pltpu.VMEM((1, group * tq), F32)],
        compiler_params=_cparams(("arbitrary", "arbitrary", "arbitrary")),
        name="flash",
    )(q, q, k, vt)


def _merge_kernel(oa_ref, ob_ref, om_ref, g0_ref, g1_ref, g2_ref, x_ref, wa_ref, wb_ref, wm_ref,
                  wo_ref, gn_ref, h_ref, n2_ref):
    merged = g0_ref[...].astype(F32) * _dot(oa_ref[...], wa_ref[...])
    merged = merged + g1_ref[...].astype(F32) * _dot(ob_ref[...], wb_ref[...])
    merged = merged + g2_ref[...].astype(F32) * _dot(om_ref[...], wm_ref[...])
    h = x_ref[...] + _dot(merged.astype(BF16), wo_ref[...])
    h_ref[...] = h
    n2_ref[...] = _rms(h, gn_ref[...]).astype(n2_ref.dtype)


def _merge(oa, ob, om, gates, x, wa, wb, wm, wo, gn, tm=256):
    t, d = x.shape
    const = lambda i: (0, 0)
    row = lambda a: pl.BlockSpec((tm, a.shape[1]), lambda i: (i, 0))
    wspec = lambda a: pl.BlockSpec(a.shape, const, pipeline_mode=pl.Buffered(1))
    gate = lambda br: pl.BlockSpec((tm, d), lambda i: (i, br))
    return pl.pallas_call(
        _merge_kernel,
        grid=(t // tm,),
        in_specs=[row(oa), row(ob), row(om), gate(0), gate(1), gate(2), row(x),
                  wspec(wa), wspec(wb), wspec(wm), wspec(wo), pl.BlockSpec((1, d), const)],
        out_specs=[pl.BlockSpec((tm, d), lambda i: (i, 0)), pl.BlockSpec((tm, d), lambda i: (i, 0))],
        out_shape=[jax.ShapeDtypeStruct((t, d), F32), jax.ShapeDtypeStruct((t, d), BF16)],
        compiler_params=_cparams(("parallel",)),
        name="merge",
    )(oa, ob, om, gates, gates, gates, x, wa, wb, wm, wo, gn)


def _mlp_kernel(n_ref, h_ref, wu_ref, wd_ref, g_ref, y_ref, acc_ref, *, final_norm):
    f = pl.program_id(1)

    @pl.when(f == 0)
    def _():
        acc_ref[...] = jnp.zeros_like(acc_ref)

    u = _dot(n_ref[...], wu_ref[...])
    a = jnp.square(jnp.maximum(u, 0.0)).astype(BF16)
    acc_ref[...] += _dot(a, wd_ref[...])

    @pl.when(f == pl.num_programs(1) - 1)
    def _():
        y = h_ref[...] + acc_ref[...]
        y_ref[...] = _rms(y, g_ref[...]) if final_norm else y


def _mlp(n2, h, wu, wd, g, final_norm, tm=512, tf=1024):
    t, d = h.shape
    dff = wu.shape[1]
    return pl.pallas_call(
        functools.partial(_mlp_kernel, final_norm=final_norm),
        grid=(t // tm, dff // tf),
        in_specs=[pl.BlockSpec((tm, d), lambda i, f: (i, 0)),
                  pl.BlockSpec((tm, d), lambda i, f: (i, 0)),
                  pl.BlockSpec((d, tf), lambda i, f: (0, f)),
                  pl.BlockSpec((tf, d), lambda i, f: (f, 0)),
                  pl.BlockSpec((1, d), lambda i, f: (0, 0))],
        out_specs=pl.BlockSpec((tm, d), lambda i, f: (i, 0)),
        out_shape=jax.ShapeDtypeStruct((t, d), F32),
        scratch_shapes=[pltpu.VMEM((tm, d), F32)],
        compiler_params=_cparams(("parallel", "arbitrary")),
        name="mlp",
    )(n2, h, wu, wd, g)


def _half_split_perm(w, dim):
    q = dim // 4
    r_lo, r_hi, c_lo, c_hi = (w[..., i * q:(i + 1) * q] for i in range(4))
    pad = jnp.zeros(w.shape[:-1] + (LANES // 2 - 2 * q,), w.dtype)
    return jnp.concatenate([r_lo, c_lo, pad, r_hi, c_hi, pad], axis=-1)


def _rope_tables(seq, dim):
    n_rows = seq // GRID_W
    h = dim // 2
    inv_freq = ROPE_THETA ** (-jnp.arange(0, h, 2, dtype=F32) / h)

    def cs(n_pos):
        ang = jnp.arange(n_pos, dtype=jnp.int32).astype(F32)[:, None] * inv_freq[None, :]
        return jnp.cos(ang), jnp.sin(ang)

    def per_token(tab, axis):
        tab = jnp.expand_dims(tab, 1 - axis)
        return jnp.broadcast_to(tab, (n_rows, GRID_W, tab.shape[-1])).reshape(seq, -1)

    (cr, sr), (cc, sc) = cs(n_rows), cs(GRID_W)
    cr, sr, cc, sc = per_token(cr, 0), per_token(sr, 0), per_token(cc, 1), per_token(sc, 1)
    pad = jnp.zeros((seq, LANES // 2 - dim // 2), F32)
    cos = jnp.concatenate([cr, cc, pad, cr, cc, pad], axis=1)
    sin = jnp.concatenate([-sr, -sc, pad, sr, sc, pad], axis=1)
    return cos, sin


def _prep_layer(l, w_in, g_qa, g_ka, w_q_b, w_kv_b, w_mem_kv, w_br_a, w_br_b, w_br_m, w_out, w_up,
                w_down):
    d = w_in.shape[1]
    wi = w_in[l]
    nq, nk = HEADS_A * HEAD_DIM_A, KV_HEADS_A * HEAD_DIM_A
    lat_w = Q_LORA_B + KV_LORA_B + ROPE_DIM_B
    qm_w = HEADS_M * HEAD_DIM_M
    c_lat = nq + 2 * nk
    c_qm, c_gate = c_lat + lat_w, c_lat + lat_w + qm_w
    c_rope = c_lat + Q_LORA_B + KV_LORA_B

    def perm_heads(w, n_heads):
        return _half_split_perm(w.reshape(d, n_heads, HEAD_DIM_A), HEAD_DIM_A).reshape(d, -1)

    w_a = jnp.concatenate([perm_heads(wi[:, :nq], HEADS_A), perm_heads(wi[:, nq:nq + nk], KV_HEADS_A),
                           wi[:, nq + nk:c_lat], wi[:, c_qm:c_gate]], axis=1).astype(BF16)
    g_a = jnp.concatenate(
        [jnp.tile(_half_split_perm(g_qa[l], HEAD_DIM_A) * (HEAD_DIM_A ** -0.5 * LOG2E), HEADS_A),
         jnp.tile(_half_split_perm(g_ka[l], HEAD_DIM_A), KV_HEADS_A)])[None]
    w_lat = jnp.concatenate([wi[:, c_lat:c_rope], _half_split_perm(wi[:, c_rope:c_qm], ROPE_DIM_B)],
                            axis=1).astype(BF16)
    w_gate = wi[:, c_gate:].astype(BF16)
    wq = w_q_b[l].reshape(Q_LORA_B, HEADS_B, NOPE_DIM_B + ROPE_DIM_B)
    wq = jnp.concatenate([wq[:, :, :NOPE_DIM_B], _half_split_perm(wq[:, :, NOPE_DIM_B:], ROPE_DIM_B)],
                         axis=-1).reshape(Q_LORA_B, HEADS_B * QB_PAD).astype(BF16)
    wkv = w_kv_b[l].reshape(KV_LORA_B, HEADS_B, NOPE_DIM_B + V_DIM_B)
    wkv = jnp.concatenate([wkv[:, :, :NOPE_DIM_B].reshape(KV_LORA_B, -1),
                           wkv[:, :, NOPE_DIM_B:].reshape(KV_LORA_B, -1)], axis=1).astype(BF16)
    return dict(w_a=w_a, g_a=g_a, w_lat=w_lat, w_gate=w_gate, wq=wq, wkv=wkv,
                w_mem=w_mem_kv[l].astype(BF16), w_br_a=w_br_a[l].astype(BF16),
                w_br_b=w_br_b[l].astype(BF16), w_br_m=w_br_m[l].astype(BF16),
                w_out=w_out[l].astype(BF16), w_up=w_up[l].astype(BF16),
                w_down=w_down[l].astype(BF16))


def _trunk(x, mem, layers, gains, g_final):
    b, s, d = x.shape
    t = b * s
    n_mem = mem.shape[1]
    tabs_a = _rope_tables(s, HEAD_DIM_A)
    tabs_b = _rope_tables(s, ROPE_DIM_B)
    h = x.reshape(t, d)
    mem2 = mem.reshape(b * n_mem, d)
    depth = len(layers)
    tk = min(1024, s // 4)
    for l, (w, g) in enumerate(zip(layers, gains)):
        n = _norm(h, g["g_mix"])
        qk_a, vt_a, q_m = _qkv_a(n, w["w_a"], w["g_a"], tabs_a, s)
        qk_a = qk_a.reshape(b, s, -1)
        q_b, k_b, vt_b = _latent(n, w["w_lat"], g["g_cq"], g["g_ckv"], w["wq"], w["wkv"], tabs_b, s)
        gates = _gates(n, w["w_gate"])
        k_m, vt_m = _mem_kv(mem2, g["g_mem"], w["w_mem"])

        o_a = _flash(qk_a, qk_a, vt_a, n_kv_heads=KV_HEADS_A, group=HEADS_A // KV_HEADS_A,
                     dq=HEAD_DIM_A, dv=HEAD_DIM_A, q_col=0, k_col=HEADS_A, v_row=0, tq=256, tk=tk)
        o_b = _flash(q_b.reshape(b, s, -1), k_b.reshape(b, s, -1), vt_b,
                     n_kv_heads=HEADS_B, group=1, dq=QB_PAD, dv=V_DIM_B, q_col=0, k_col=0, v_row=0,
                     tq=1024, tk=tk)
        o_m = _flash(q_m.reshape(b, s, -1), k_m.reshape(b, n_mem, -1), vt_m, n_kv_heads=HEADS_M,
                     group=1, dq=HEAD_DIM_M, dv=HEAD_DIM_M, q_col=0, k_col=0, v_row=0, tq=1024,
                     tk=n_mem)

        h, n2 = _merge(o_a.reshape(t, -1), o_b.reshape(t, -1), o_m.reshape(t, -1), gates, h,
                       w["w_br_a"], w["w_br_b"], w["w_br_m"], w["w_out"], g["g_mlp"])
        last = l == depth - 1
        h = _mlp(n2, h, w["w_up"], w["w_down"], g_final if last else g["g_mlp"], final_norm=last)
    return h.reshape(b, s, d)


def kernel(x_prompt, x_sample, mem_prompt, mem_sample, g_mix, w_in, g_qa, g_ka, g_cq, w_q_b, g_ckv,
           w_kv_b, g_mem, w_mem_kv, w_br_a, w_br_b, w_br_m, w_out, g_mlp, w_up, w_down, g_final):
    depth = w_in.shape[0]
    layers = [_prep_layer(l, w_in, g_qa, g_ka, w_q_b, w_kv_b, w_mem_kv, w_br_a, w_br_b, w_br_m,
                          w_out, w_up, w_down) for l in range(depth)]
    gains = [dict(g_mix=g_mix[l][None], g_cq=g_cq[l][None], g_ckv=g_ckv[l][None],
                  g_mem=g_mem[l][None], g_mlp=g_mlp[l][None]) for l in range(depth)]
    gf = g_final[None]
    y_prompt = _trunk(x_prompt, mem_prompt, layers, gains, gf)
    y_sample = _trunk(x_sample, mem_sample, layers, gains, gf)
    return (y_prompt, y_sample)
```

```python
import functools
import math

import jax
import jax.numpy as jnp
from jax import lax
from jax.experimental import pallas as pl
from jax.experimental.pallas import tpu as pltpu

F32 = jnp.float32
BF16 = jnp.bfloat16

EPS = 1e-6
ROPE_THETA = 10000.0
GRID_W = 64
LOG2E = math.log2(math.e)

HEADS_A, KV_HEADS_A, HEAD_DIM_A = 8, 2, 128
HEADS_B, Q_LORA_B, KV_LORA_B = 8, 512, 256
NOPE_DIM_B, ROPE_DIM_B, V_DIM_B = 128, 64, 128
HEADS_M, HEAD_DIM_M = 4, 128
N_BRANCH = 3
LANES = 128
QB_PAD = 256

VMEM_LIMIT = 56 * 1024 * 1024


def _cparams(sem):
    return pltpu.CompilerParams(dimension_semantics=sem, vmem_limit_bytes=VMEM_LIMIT)


def _rms(x, g):
    y = x * lax.rsqrt(jnp.mean(x * x, axis=-1, keepdims=True) + EPS)
    return y * g


def _dot(a, b):
    return jnp.dot(a, b, preferred_element_type=F32)


def _rope(x, cos, sin):
    return x * cos + pltpu.roll(x, LANES // 2, 1) * sin


def _qkv_a_kernel(x_ref, gx_ref, w_ref, g_ref, cos_ref, sin_ref, n_ref, qk_ref, vt_ref, qm_ref, *,
                  qm_scale):
    n = _rms(x_ref[...], gx_ref[...]).astype(n_ref.dtype)
    n_ref[...] = n
    z = _dot(n, w_ref[...])
    cos, sin = cos_ref[...], sin_ref[...]
    n_qk = qk_ref.shape[1]
    n_v = vt_ref.shape[0]
    for h in range(n_qk // LANES):
        c = slice(h * LANES, (h + 1) * LANES)
        qk_ref[:, c] = _rope(_rms(z[:, c], g_ref[:, c]), cos, sin).astype(qk_ref.dtype)
    vt_ref[...] = z[:, n_qk:n_qk + n_v].T.astype(vt_ref.dtype)
    qm_ref[...] = (z[:, n_qk + n_v:] * qm_scale).astype(qm_ref.dtype)


def _qkv_a(x, gx, w, g, tabs, seq, tm=512):
    t, d = x.shape
    n_qk = (HEADS_A + KV_HEADS_A) * HEAD_DIM_A
    n_v = KV_HEADS_A * HEAD_DIM_A
    n_qm = HEADS_M * HEAD_DIM_M
    cos, sin = tabs
    nblk = seq // tm
    const = lambda i: (0, 0)
    tab_spec = pl.BlockSpec((tm, LANES), lambda i: (i % nblk, 0))
    return pl.pallas_call(
        functools.partial(_qkv_a_kernel, qm_scale=HEAD_DIM_M ** -0.5 * LOG2E),
        grid=(t // tm,),
        in_specs=[pl.BlockSpec((tm, d), lambda i: (i, 0)),
                  pl.BlockSpec(gx.shape, const),
                  pl.BlockSpec(w.shape, const, pipeline_mode=pl.Buffered(1)),
                  pl.BlockSpec(g.shape, const), tab_spec, tab_spec],
        out_specs=[pl.BlockSpec((tm, d), lambda i: (i, 0)),
                   pl.BlockSpec((tm, n_qk), lambda i: (i, 0)),
                   pl.BlockSpec((n_v, tm), lambda i: (0, i)),
                   pl.BlockSpec((tm, n_qm), lambda i: (i, 0))],
        out_shape=[jax.ShapeDtypeStruct((t, d), BF16),
                   jax.ShapeDtypeStruct((t, n_qk), BF16),
                   jax.ShapeDtypeStruct((n_v, t), BF16),
                   jax.ShapeDtypeStruct((t, n_qm), BF16)],
        compiler_params=_cparams(("parallel",)),
        name="qkv_a",
    )(x, gx, w, g, cos, sin)


def _latent_kernel(n_ref, wl_ref, gcq_ref, gckv_ref, wq_ref, wkv_ref, cos_ref, sin_ref,
                   q_ref, k_ref, vt_ref, *, q_scale):
    lat = _dot(n_ref[...], wl_ref[...])
    cos, sin = cos_ref[...], sin_ref[...]
    cq = _rms(lat[:, :Q_LORA_B], gcq_ref[...]).astype(BF16)
    qb = _dot(cq, wq_ref[...])
    for h in range(HEADS_B):
        c0 = h * QB_PAD
        q_ref[:, c0:c0 + LANES] = (qb[:, c0:c0 + LANES] * q_scale).astype(q_ref.dtype)
        q_ref[:, c0 + LANES:c0 + QB_PAD] = (
            _rope(qb[:, c0 + LANES:c0 + QB_PAD], cos, sin) * q_scale).astype(q_ref.dtype)
    ckv = _rms(lat[:, Q_LORA_B:Q_LORA_B + KV_LORA_B], gckv_ref[...]).astype(BF16)
    kvb = _dot(ckv, wkv_ref[...])
    kpe = _rope(lat[:, Q_LORA_B + KV_LORA_B:], cos, sin).astype(k_ref.dtype)
    for h in range(HEADS_B):
        c0 = h * QB_PAD
        k_ref[:, c0:c0 + LANES] = kvb[:, h * LANES:(h + 1) * LANES].astype(k_ref.dtype)
        k_ref[:, c0 + LANES:c0 + QB_PAD] = kpe
    vt_ref[...] = kvb[:, HEADS_B * NOPE_DIM_B:].T.astype(vt_ref.dtype)


def _latent(n, wl, gcq, gckv, wq, wkv, tabs, seq, tm=256):
    t, d = n.shape
    cos, sin = tabs
    nblk = seq // tm
    const = lambda i: (0, 0)
    tab_spec = pl.BlockSpec((tm, LANES), lambda i: (i % nblk, 0))
    row = lambda w: pl.BlockSpec((tm, w), lambda i: (i, 0))
    q_scale = (NOPE_DIM_B + ROPE_DIM_B) ** -0.5 * LOG2E
    return pl.pallas_call(
        functools.partial(_latent_kernel, q_scale=q_scale),
        grid=(t // tm,),
        in_specs=[row(d),
                  pl.BlockSpec(wl.shape, const), pl.BlockSpec(gcq.shape, const),
                  pl.BlockSpec(gckv.shape, const), pl.BlockSpec(wq.shape, const),
                  pl.BlockSpec(wkv.shape, const), tab_spec, tab_spec],
        out_specs=[row(HEADS_B * QB_PAD), row(HEADS_B * QB_PAD),
                   pl.BlockSpec((HEADS_B * V_DIM_B, tm), lambda i: (0, i))],
        out_shape=[jax.ShapeDtypeStruct((t, HEADS_B * QB_PAD), BF16),
                   jax.ShapeDtypeStruct((t, HEADS_B * QB_PAD), BF16),
                   jax.ShapeDtypeStruct((HEADS_B * V_DIM_B, t), BF16)],
        compiler_params=_cparams(("parallel",)),
        name="latent",
    )(n, wl, gcq, gckv, wq, wkv, cos, sin)


def _gate_kernel(n_ref, w_ref, o_ref):
    o_ref[...] = jax.nn.sigmoid(_dot(n_ref[...], w_ref[...])).astype(o_ref.dtype)


def _gates(n, w, tm=512, tn=2048):
    t, d = n.shape
    nn = w.shape[1]
    return pl.pallas_call(
        _gate_kernel,
        grid=(nn // tn, t // tm),
        in_specs=[pl.BlockSpec((tm, d), lambda j, i: (i, 0)),
                  pl.BlockSpec((d, tn), lambda j, i: (0, j))],
        out_specs=pl.BlockSpec((tm, tn), lambda j, i: (i, j)),
        out_shape=jax.ShapeDtypeStruct((t, nn), BF16),
        compiler_params=_cparams(("arbitrary", "arbitrary")),
        name="gates",
    )(n, w)


def _mem_kv_kernel(m_ref, g_ref, w_ref, k_ref, vt_ref):
    nm = _rms(m_ref[...], g_ref[...]).astype(BF16)
    z = _dot(nm, w_ref[...])
    nk = k_ref.shape[1]
    k_ref[...] = z[:, :nk].astype(k_ref.dtype)
    vt_ref[...] = z[:, nk:].T.astype(vt_ref.dtype)


def _mem_kv(mem, g, w, tm=256):
    t, d = mem.shape
    nn = w.shape[1] // 2
    return pl.pallas_call(
        _mem_kv_kernel,
        grid=(t // tm,),
        in_specs=[pl.BlockSpec((tm, d), lambda i: (i, 0)),
                  pl.BlockSpec((1, d), lambda i: (0, 0)),
                  pl.BlockSpec((d, 2 * nn), lambda i: (0, 0))],
        out_specs=[pl.BlockSpec((tm, nn), lambda i: (i, 0)),
                   pl.BlockSpec((nn, tm), lambda i: (0, i))],
        out_shape=[jax.ShapeDtypeStruct((t, nn), BF16), jax.ShapeDtypeStruct((nn, t), BF16)],
        compiler_params=_cparams(("parallel",)),
        name="mem_kv",
    )(mem, g, w)


def _flash_kernel(q_ref, qn_ref, k_ref, vt_ref, o_ref, s_ref, c_ref, *, group, dq, dv, tk):
    tq = q_ref.shape[0]
    nk = k_ref.shape[0] // tk
    cols = group * tq

    def stacked(ref):
        if group == 1:
            return ref[...]
        return jnp.concatenate([ref[:, g * dq:(g + 1) * dq] for g in range(group)], axis=0)

    q = stacked(q_ref)

    def scores(qs, j):
        ks = k_ref[pl.ds(pl.multiple_of(j * tk, tk), tk), :]
        s = lax.dot_general(ks, qs, (((1,), (1,)), ((), ())), preferred_element_type=F32)
        return s, jnp.max(s, axis=0, keepdims=True)

    def consume(j, s, s_max, m, l, acc):
        vt = vt_ref[:, pl.ds(pl.multiple_of(j * tk, tk), tk)]
        m_new = jnp.maximum(m, s_max)
        alpha = jnp.exp2(m - m_new)
        p = jnp.exp2(s - m_new)
        l_new = alpha * l + jnp.sum(p, axis=0, keepdims=True)
        acc_new = alpha * acc + _dot(vt, p.astype(BF16))
        return m_new, l_new, acc_new

    state = (jnp.full((1, cols), -jnp.inf, F32), jnp.zeros((1, cols), F32),
             jnp.zeros((dv, cols), F32))
    if nk == 1:
        _, l, acc = consume(0, *scores(q, 0), *state)
    else:
        assert nk % 2 == 0

        @pl.when(pl.program_id(2) == 0)
        def _():
            s0, c0 = scores(q, 0)
            s_ref[0] = s0
            c_ref[...] = c0

        def pair(i, carry):
            m, l, acc, c_even = carry
            j = 2 * i
            s_odd, c_odd = scores(q, j + 1)
            s_ref[1] = s_odd
            m, l, acc = consume(j, s_ref[0], c_even, m, l, acc)
            s_even, c_next = scores(q, j + 2)
            s_ref[0] = s_even
            m, l, acc = consume(j + 1, s_ref[1], c_odd, m, l, acc)
            return m, l, acc, c_next

        m, l, acc, c_even = lax.fori_loop(0, nk // 2 - 1, pair, state + (c_ref[...],))
        s_odd, c_odd = scores(q, nk - 1)
        s_ref[1] = s_odd
        m, l, acc = consume(nk - 2, s_ref[0], c_even, m, l, acc)
        s_next, c_next = scores(stacked(qn_ref), 0)
        s_ref[0] = s_next
        c_ref[...] = c_next
        _, l, acc = consume(nk - 1, s_ref[1], c_odd, m, l, acc)
    o = acc / l
    for g in range(group):
        o_ref[:, g * dv:(g + 1) * dv] = o[:, g * tq:(g + 1) * tq].T.astype(o_ref.dtype)


def _flash(q, k, vt, *, n_kv_heads, group, dq, dv, q_col, k_col, v_row, tq, tk):
    b, s, _ = q.shape
    sk = k.shape[1]
    tk = min(tk, sk)
    nq = s // tq
    q_block = (None, tq, group * dq)
    return pl.pallas_call(
        functools.partial(_flash_kernel, group=group, dq=dq, dv=dv, tk=tk),
        grid=(b, n_kv_heads, nq),
        in_specs=[pl.BlockSpec(q_block, lambda bi, h, qi: (bi, qi, q_col + h)),
                  pl.BlockSpec(q_block, lambda bi, h, qi: (bi, jnp.minimum(qi + 1, nq - 1), q_col + h)),
                  pl.BlockSpec((None, sk, dq), lambda bi, h, qi: (bi, 0, k_col + h)),
                  pl.BlockSpec((dv, sk), lambda bi, h, qi: (v_row + h, bi))],
        out_specs=pl.BlockSpec((None, tq, group * dv), lambda bi, h, qi: (bi, qi, h)),
        out_shape=jax.ShapeDtypeStruct((b, s, n_kv_heads * group * dv), BF16),
        scratch_shapes=[pltpu.VMEM((2, tk, group * tq), F32), pltpu.VMEM((1, group * tq), F32)],
        compiler_params=_cparams(("arbitrary", "arbitrary", "arbitrary")),
        name="flash",
    )(q, q, k, vt)


def _merge_kernel(oa_ref, ob_ref, om_ref, g0_ref, g1_ref, g2_ref, x_ref, wa_ref, wb_ref, wm_ref,
                  wo_ref, gn_ref, h_ref, n2_ref):
    merged = g0_ref[...].astype(F32) * _dot(oa_ref[...], wa_ref[...])
    merged = merged + g1_ref[...].astype(F32) * _dot(ob_ref[...], wb_ref[...])
    merged = merged + g2_ref[...].astype(F32) * _dot(om_ref[...], wm_ref[...])
    h = x_ref[...] + _dot(merged.astype(BF16), wo_ref[...])
    h_ref[...] = h
    n2_ref[...] = _rms(h, gn_ref[...]).astype(n2_ref.dtype)


def _merge(oa, ob, om, gates, x, wa, wb, wm, wo, gn, tm=256):
    t, d = x.shape
    const = lambda i: (0, 0)
    row = lambda a: pl.BlockSpec((tm, a.shape[1]), lambda i: (i, 0))
    wspec = lambda a: pl.BlockSpec(a.shape, const, pipeline_mode=pl.Buffered(1))
    gate = lambda br: pl.BlockSpec((tm, d), lambda i: (i, br))
    return pl.pallas_call(
        _merge_kernel,
        grid=(t // tm,),
        in_specs=[row(oa), row(ob), row(om), gate(0), gate(1), gate(2), row(x),
                  wspec(wa), wspec(wb), wspec(wm), wspec(wo), pl.BlockSpec((1, d), const)],
        out_specs=[pl.BlockSpec((tm, d), lambda i: (i, 0)), pl.BlockSpec((tm, d), lambda i: (i, 0))],
        out_shape=[jax.ShapeDtypeStruct((t, d), F32), jax.ShapeDtypeStruct((t, d), BF16)],
        compiler_params=_cparams(("parallel",)),
        name="merge",
    )(oa, ob, om, gates, gates, gates, x, wa, wb, wm, wo, gn)


def _mlp_kernel(n_ref, h_ref, wu_ref, wd_ref, g_ref, y_ref, acc_ref, *, final_norm):
    f = pl.program_id(1)

    @pl.when(f == 0)
    def _():
        acc_ref[...] = jnp.zeros_like(acc_ref)

    u = _dot(n_ref[...], wu_ref[...])
    a = jnp.square(jnp.maximum(u, 0.0)).astype(BF16)
    acc_ref[...] += _dot(a, wd_ref[...])

    @pl.when(f == pl.num_programs(1) - 1)
    def _():
        y = h_ref[...] + acc_ref[...]
        y_ref[...] = _rms(y, g_ref[...]) if final_norm else y


def _mlp(n2, h, wu, wd, g, final_norm, tm=512, tf=1024):
    t, d = h.shape
    dff = wu.shape[1]
    return pl.pallas_call(
        functools.partial(_mlp_kernel, final_norm=final_norm),
        grid=(t // tm, dff // tf),
        in_specs=[pl.BlockSpec((tm, d), lambda i, f: (i, 0)),
                  pl.BlockSpec((tm, d), lambda i, f: (i, 0)),
                  pl.BlockSpec((d, tf), lambda i, f: (0, f)),
                  pl.BlockSpec((tf, d), lambda i, f: (f, 0)),
                  pl.BlockSpec((1, d), lambda i, f: (0, 0))],
        out_specs=pl.BlockSpec((tm, d), lambda i, f: (i, 0)),
        out_shape=jax.ShapeDtypeStruct((t, d), F32),
        scratch_shapes=[pltpu.VMEM((tm, d), F32)],
        compiler_params=_cparams(("parallel", "arbitrary")),
        name="mlp",
    )(n2, h, wu, wd, g)


def _half_split_perm(w, dim):
    q = dim // 4
    r_lo, r_hi, c_lo, c_hi = (w[..., i * q:(i + 1) * q] for i in range(4))
    pad = jnp.zeros(w.shape[:-1] + (LANES // 2 - 2 * q,), w.dtype)
    return jnp.concatenate([r_lo, c_lo, pad, r_hi, c_hi, pad], axis=-1)


def _rope_tables(seq, dim):
    n_rows = seq // GRID_W
    h = dim // 2
    inv_freq = ROPE_THETA ** (-jnp.arange(0, h, 2, dtype=F32) / h)

    def cs(n_pos):
        ang = jnp.arange(n_pos, dtype=jnp.int32).astype(F32)[:, None] * inv_freq[None, :]
        return jnp.cos(ang), jnp.sin(ang)

    def per_token(tab, axis):
        tab = jnp.expand_dims(tab, 1 - axis)
        return jnp.broadcast_to(tab, (n_rows, GRID_W, tab.shape[-1])).reshape(seq, -1)

    (cr, sr), (cc, sc) = cs(n_rows), cs(GRID_W)
    cr, sr, cc, sc = per_token(cr, 0), per_token(sr, 0), per_token(cc, 1), per_token(sc, 1)
    pad = jnp.zeros((seq, LANES // 2 - dim // 2), F32)
    cos = jnp.concatenate([cr, cc, pad, cr, cc, pad], axis=1)
    sin = jnp.concatenate([-sr, -sc, pad, sr, sc, pad], axis=1)
    return cos, sin


def _prep_layer(l, w_in, g_qa, g_ka, w_q_b, w_kv_b, w_mem_kv, w_br_a, w_br_b, w_br_m, w_out, w_up,
                w_down):
    d = w_in.shape[1]
    wi = w_in[l].astype(BF16)
    nq, nk = HEADS_A * HEAD_DIM_A, KV_HEADS_A * HEAD_DIM_A
    lat_w = Q_LORA_B + KV_LORA_B + ROPE_DIM_B
    qm_w = HEADS_M * HEAD_DIM_M
    c_lat = nq + 2 * nk
    c_qm, c_gate = c_lat + lat_w, c_lat + lat_w + qm_w
    c_rope = c_lat + Q_LORA_B + KV_LORA_B

    def perm_heads(w, n_heads):
        return _half_split_perm(w.reshape(d, n_heads, HEAD_DIM_A), HEAD_DIM_A).reshape(d, -1)

    w_a = jnp.concatenate([perm_heads(wi[:, :nq], HEADS_A), perm_heads(wi[:, nq:nq + nk], KV_HEADS_A),
                           wi[:, nq + nk:c_lat], wi[:, c_qm:c_gate]], axis=1)
    g_a = jnp.concatenate(
        [jnp.tile(_half_split_perm(g_qa[l], HEAD_DIM_A) * (HEAD_DIM_A ** -0.5 * LOG2E), HEADS_A),
         jnp.tile(_half_split_perm(g_ka[l], HEAD_DIM_A), KV_HEADS_A)])[None]
    w_lat = jnp.concatenate([wi[:, c_lat:c_rope], _half_split_perm(wi[:, c_rope:c_qm], ROPE_DIM_B)],
                            axis=1)
    w_gate = wi[:, c_gate:]
    wq = w_q_b[l].astype(BF16).reshape(Q_LORA_B, HEADS_B, NOPE_DIM_B + ROPE_DIM_B)
    wq = jnp.concatenate([wq[:, :, :NOPE_DIM_B], _half_split_perm(wq[:, :, NOPE_DIM_B:], ROPE_DIM_B)],
                         axis=-1).reshape(Q_LORA_B, HEADS_B * QB_PAD)
    wkv = w_kv_b[l].astype(BF16).reshape(KV_LORA_B, HEADS_B, NOPE_DIM_B + V_DIM_B)
    wkv = jnp.concatenate([wkv[:, :, :NOPE_DIM_B].reshape(KV_LORA_B, -1),
                           wkv[:, :, NOPE_DIM_B:].reshape(KV_LORA_B, -1)], axis=1)
    return dict(w_a=w_a, g_a=g_a, w_lat=w_lat, w_gate=w_gate, wq=wq, wkv=wkv,
                w_mem=w_mem_kv[l].astype(BF16), w_br_a=w_br_a[l].astype(BF16),
                w_br_b=w_br_b[l].astype(BF16), w_br_m=w_br_m[l].astype(BF16),
                w_out=w_out[l].astype(BF16), w_up=w_up[l].astype(BF16),
                w_down=w_down[l].astype(BF16))


def _trunk(x, mem, layers, gains, g_final):
    b, s, d = x.shape
    t = b * s
    n_mem = mem.shape[1]
    tabs_a = _rope_tables(s, HEAD_DIM_A)
    tabs_b = _rope_tables(s, ROPE_DIM_B)
    h = x.reshape(t, d)
    mem2 = mem.reshape(b * n_mem, d)
    depth = len(layers)
    tk = min(1024, s // 4)
    for l, (w, g) in enumerate(zip(layers, gains)):
        n, qk_a, vt_a, q_m = _qkv_a(h, g["g_mix"], w["w_a"], w["g_a"], tabs_a, s)
        qk_a = qk_a.reshape(b, s, -1)
        q_b, k_b, vt_b = _latent(n, w["w_lat"], g["g_cq"], g["g_ckv"], w["wq"], w["wkv"], tabs_b, s)
        gates = _gates(n, w["w_gate"])
        k_m, vt_m = _mem_kv(mem2, g["g_mem"], w["w_mem"])

        o_a = _flash(qk_a, qk_a, vt_a, n_kv_heads=KV_HEADS_A, group=HEADS_A // KV_HEADS_A,
                     dq=HEAD_DIM_A, dv=HEAD_DIM_A, q_col=0, k_col=HEADS_A, v_row=0, tq=512, tk=tk)
        o_b = _flash(q_b.reshape(b, s, -1), k_b.reshape(b, s, -1), vt_b,
                     n_kv_heads=HEADS_B, group=1, dq=QB_PAD, dv=V_DIM_B, q_col=0, k_col=0, v_row=0,
                     tq=min(2048, s // 2), tk=tk)
        o_m = _flash(q_m.reshape(b, s, -1), k_m.reshape(b, n_mem, -1), vt_m, n_kv_heads=HEADS_M,
                     group=1, dq=HEAD_DIM_M, dv=HEAD_DIM_M, q_col=0, k_col=0, v_row=0, tq=1024,
                     tk=n_mem)

        h, n2 = _merge(o_a.reshape(t, -1), o_b.reshape(t, -1), o_m.reshape(t, -1), gates, h,
                       w["w_br_a"], w["w_br_b"], w["w_br_m"], w["w_out"], g["g_mlp"])
        last = l == depth - 1
        h = _mlp(n2, h, w["w_up"], w["w_down"], g_final if last else g["g_mlp"], final_norm=last)
    return h.reshape(b, s, d)


def kernel(x_prompt, x_sample, mem_prompt, mem_sample, g_mix, w_in, g_qa, g_ka, g_cq, w_q_b, g_ckv,
           w_kv_b, g_mem, w_mem_kv, w_br_a, w_br_b, w_br_m, w_out, g_mlp, w_up, w_down, g_final):
    depth = w_in.shape[0]
    layers = [_prep_layer(l, w_in, g_qa, g_ka, w_q_b, w_kv_b, w_mem_kv, w_br_a, w_br_b, w_br_m,
                          w_out, w_up, w_down) for l in range(depth)]
    gains = [dict(g_mix=g_mix[l][None], g_cq=g_cq[l][None], g_ckv=g_ckv[l][None],
                  g_mem=g_mem[l][None], g_mlp=g_mlp[l][None]) for l in range(depth)]
    gf = g_final[None]
    y_prompt = _trunk(x_prompt, mem_prompt, layers, gains, gf)
    y_sample = _trunk(x_sample, mem_sample, layers, gains, gf)
    return (y_prompt, y_sample)
```

```python
import functools
import math

import jax
import jax.numpy as jnp
from jax import lax
from jax.experimental import pallas as pl
from jax.experimental.pallas import tpu as pltpu

F32 = jnp.float32
BF16 = jnp.bfloat16

EPS = 1e-6
ROPE_THETA = 10000.0
GRID_W = 64
LOG2E = math.log2(math.e)

HEADS_A, KV_HEADS_A, HEAD_DIM_A = 8, 2, 128
HEADS_B, Q_LORA_B, KV_LORA_B = 8, 512, 256
NOPE_DIM_B, ROPE_DIM_B, V_DIM_B = 128, 64, 128
HEADS_M, HEAD_DIM_M = 4, 128
N_BRANCH = 3
LANES = 128
QB_PAD = 256

VMEM_LIMIT = 56 * 1024 * 1024


def _cparams(sem):
    return pltpu.CompilerParams(dimension_semantics=sem, vmem_limit_bytes=VMEM_LIMIT)


def _rms(x, g):
    y = x * lax.rsqrt(jnp.mean(x * x, axis=-1, keepdims=True) + EPS)
    return y * g


def _dot(a, b):
    return jnp.dot(a, b, preferred_element_type=F32)


def _rope(x, cos, sin):
    return x * cos + pltpu.roll(x, LANES // 2, 1) * sin


def _qkv_a_kernel(x_ref, gx_ref, w_ref, g_ref, cos_ref, sin_ref, n_ref, qk_ref, vt_ref, qm_ref, *,
                  qm_scale):
    n = _rms(x_ref[...], gx_ref[...]).astype(n_ref.dtype)
    n_ref[...] = n
    z = _dot(n, w_ref[...])
    cos, sin = cos_ref[...], sin_ref[...]
    n_qk = qk_ref.shape[1]
    n_v = vt_ref.shape[0]
    for h in range(n_qk // LANES):
        c = slice(h * LANES, (h + 1) * LANES)
        qk_ref[:, c] = _rope(_rms(z[:, c], g_ref[:, c]), cos, sin).astype(qk_ref.dtype)
    vt_ref[...] = z[:, n_qk:n_qk + n_v].T.astype(vt_ref.dtype)
    qm_ref[...] = (z[:, n_qk + n_v:] * qm_scale).astype(qm_ref.dtype)


def _qkv_a(x, gx, w, g, tabs, seq, tm=512):
    t, d = x.shape
    n_qk = (HEADS_A + KV_HEADS_A) * HEAD_DIM_A
    n_v = KV_HEADS_A * HEAD_DIM_A
    n_qm = HEADS_M * HEAD_DIM_M
    cos, sin = tabs
    nblk = seq // tm
    const = lambda i: (0, 0)
    tab_spec = pl.BlockSpec((tm, LANES), lambda i: (i % nblk, 0))
    return pl.pallas_call(
        functools.partial(_qkv_a_kernel, qm_scale=HEAD_DIM_M ** -0.5 * LOG2E),
        grid=(t // tm,),
        in_specs=[pl.BlockSpec((tm, d), lambda i: (i, 0)),
                  pl.BlockSpec(gx.shape, const),
                  pl.BlockSpec(w.shape, const, pipeline_mode=pl.Buffered(1)),
                  pl.BlockSpec(g.shape, const), tab_spec, tab_spec],
        out_specs=[pl.BlockSpec((tm, d), lambda i: (i, 0)),
                   pl.BlockSpec((tm, n_qk), lambda i: (i, 0)),
                   pl.BlockSpec((n_v, tm), lambda i: (0, i)),
                   pl.BlockSpec((tm, n_qm), lambda i: (i, 0))],
        out_shape=[jax.ShapeDtypeStruct((t, d), BF16),
                   jax.ShapeDtypeStruct((t, n_qk), BF16),
                   jax.ShapeDtypeStruct((n_v, t), BF16),
                   jax.ShapeDtypeStruct((t, n_qm), BF16)],
        compiler_params=_cparams(("parallel",)),
        name="qkv_a",
    )(x, gx, w, g, cos, sin)


def _latent_kernel(n_ref, wl_ref, gcq_ref, gckv_ref, wq_ref, wkv_ref, cos_ref, sin_ref,
                   q_ref, k_ref, vt_ref, *, q_scale):
    lat = _dot(n_ref[...], wl_ref[...])
    cos, sin = cos_ref[...], sin_ref[...]
    cq = _rms(lat[:, :Q_LORA_B], gcq_ref[...]).astype(BF16)
    qb = _dot(cq, wq_ref[...])
    for h in range(HEADS_B):
        c0 = h * QB_PAD
        q_ref[:, c0:c0 + LANES] = (qb[:, c0:c0 + LANES] * q_scale).astype(q_ref.dtype)
        q_ref[:, c0 + LANES:c0 + QB_PAD] = (
            _rope(qb[:, c0 + LANES:c0 + QB_PAD], cos, sin) * q_scale).astype(q_ref.dtype)
    ckv = _rms(lat[:, Q_LORA_B:Q_LORA_B + KV_LORA_B], gckv_ref[...]).astype(BF16)
    kvb = _dot(ckv, wkv_ref[...])
    kpe = _rope(lat[:, Q_LORA_B + KV_LORA_B:], cos, sin).astype(k_ref.dtype)
    for h in range(HEADS_B):
        c0 = h * QB_PAD
        k_ref[:, c0:c0 + LANES] = kvb[:, h * LANES:(h + 1) * LANES].astype(k_ref.dtype)
        k_ref[:, c0 + LANES:c0 + QB_PAD] = kpe
    vt_ref[...] = kvb[:, HEADS_B * NOPE_DIM_B:].T.astype(vt_ref.dtype)


def _latent(n, wl, gcq, gckv, wq, wkv, tabs, seq, tm=512):
    t, d = n.shape
    cos, sin = tabs
    nblk = seq // tm
    const = lambda i: (0, 0)
    tab_spec = pl.BlockSpec((tm, LANES), lambda i: (i % nblk, 0))
    row = lambda w: pl.BlockSpec((tm, w), lambda i: (i, 0))
    q_scale = (NOPE_DIM_B + ROPE_DIM_B) ** -0.5 * LOG2E
    return pl.pallas_call(
        functools.partial(_latent_kernel, q_scale=q_scale),
        grid=(t // tm,),
        in_specs=[row(d),
                  pl.BlockSpec(wl.shape, const), pl.BlockSpec(gcq.shape, const),
                  pl.BlockSpec(gckv.shape, const), pl.BlockSpec(wq.shape, const),
                  pl.BlockSpec(wkv.shape, const), tab_spec, tab_spec],
        out_specs=[row(HEADS_B * QB_PAD), row(HEADS_B * QB_PAD),
                   pl.BlockSpec((HEADS_B * V_DIM_B, tm), lambda i: (0, i))],
        out_shape=[jax.ShapeDtypeStruct((t, HEADS_B * QB_PAD), BF16),
                   jax.ShapeDtypeStruct((t, HEADS_B * QB_PAD), BF16),
                   jax.ShapeDtypeStruct((HEADS_B * V_DIM_B, t), BF16)],
        compiler_params=_cparams(("parallel",)),
        name="latent",
    )(n, wl, gcq, gckv, wq, wkv, cos, sin)


def _gate_kernel(n_ref, w_ref, o_ref):
    o_ref[...] = jax.nn.sigmoid(_dot(n_ref[...], w_ref[...])).astype(o_ref.dtype)


def _gates(n, w, tm=512, tn=2048):
    t, d = n.shape
    nn = w.shape[1]
    return pl.pallas_call(
        _gate_kernel,
        grid=(nn // tn, t // tm),
        in_specs=[pl.BlockSpec((tm, d), lambda j, i: (i, 0)),
                  pl.BlockSpec((d, tn), lambda j, i: (0, j))],
        out_specs=pl.BlockSpec((tm, tn), lambda j, i: (i, j)),
        out_shape=jax.ShapeDtypeStruct((t, nn), BF16),
        compiler_params=_cparams(("arbitrary", "arbitrary")),
        name="gates",
    )(n, w)


def _mem_kv_kernel(m_ref, g_ref, w_ref, k_ref, vt_ref):
    nm = _rms(m_ref[...], g_ref[...]).astype(BF16)
    z = _dot(nm, w_ref[...])
    nk = k_ref.shape[1]
    k_ref[...] = z[:, :nk].astype(k_ref.dtype)
    vt_ref[...] = z[:, nk:].T.astype(vt_ref.dtype)


def _mem_kv(mem, g, w, tm=256):
    t, d = mem.shape
    nn = w.shape[1] // 2
    return pl.pallas_call(
        _mem_kv_kernel,
        grid=(t // tm,),
        in_specs=[pl.BlockSpec((tm, d), lambda i: (i, 0)),
                  pl.BlockSpec((1, d), lambda i: (0, 0)),
                  pl.BlockSpec((d, 2 * nn), lambda i: (0, 0))],
        out_specs=[pl.BlockSpec((tm, nn), lambda i: (i, 0)),
                   pl.BlockSpec((nn, tm), lambda i: (0, i))],
        out_shape=[jax.ShapeDtypeStruct((t, nn), BF16), jax.ShapeDtypeStruct((nn, t), BF16)],
        compiler_params=_cparams(("parallel",)),
        name="mem_kv",
    )(mem, g, w)


def _flash_kernel(q_ref, qn_ref, k_ref, vt_ref, o_ref, s_ref, c_ref, *, group, dq, dv, tk):
    tq = q_ref.shape[0]
    nk = k_ref.shape[0] // tk
    cols = group * tq

    def stacked(ref):
        if group == 1:
            return ref[...]
        return jnp.concatenate([ref[:, g * dq:(g + 1) * dq] for g in range(group)], axis=0)

    q = stacked(q_ref)

    def scores(qs, j):
        ks = k_ref[pl.ds(pl.multiple_of(j * tk, tk), tk), :]
        s = lax.dot_general(ks, qs, (((1,), (1,)), ((), ())), preferred_element_type=F32)
        return s, jnp.max(s, axis=0, keepdims=True)

    def consume(j, s, s_max, m, l, acc):
        vt = vt_ref[:, pl.ds(pl.multiple_of(j * tk, tk), tk)]
        m_new = jnp.maximum(m, s_max)
        alpha = jnp.exp2(m - m_new)
        p = jnp.exp2(s - m_new)
        l_new = alpha * l + jnp.sum(p, axis=0, keepdims=True)
        acc_new = alpha * acc + _dot(vt, p.astype(BF16))
        return m_new, l_new, acc_new

    state = (jnp.full((1, cols), -jnp.inf, F32), jnp.zeros((1, cols), F32),
             jnp.zeros((dv, cols), F32))
    if nk == 1:
        _, l, acc = consume(0, *scores(q, 0), *state)
    else:
        assert nk % 2 == 0

        @pl.when(pl.program_id(2) == 0)
        def _():
            s0, c0 = scores(q, 0)
            s_ref[0] = s0
            c_ref[...] = c0

        def pair(i, carry):
            m, l, acc, c_even = carry
            j = 2 * i
            s_odd, c_odd = scores(q, j + 1)
            s_ref[1] = s_odd
            m, l, acc = consume(j, s_ref[0], c_even, m, l, acc)
            s_even, c_next = scores(q, j + 2)
            s_ref[0] = s_even
            m, l, acc = consume(j + 1, s_ref[1], c_odd, m, l, acc)
            return m, l, acc, c_next

        m, l, acc, c_even = lax.fori_loop(0, nk // 2 - 1, pair, state + (c_ref[...],))
        s_odd, c_odd = scores(q, nk - 1)
        s_ref[1] = s_odd
        m, l, acc = consume(nk - 2, s_ref[0], c_even, m, l, acc)
        s_next, c_next = scores(stacked(qn_ref), 0)
        s_ref[0] = s_next
        c_ref[...] = c_next
        _, l, acc = consume(nk - 1, s_ref[1], c_odd, m, l, acc)
    o = acc / l
    for g in range(group):
        o_ref[:, g * dv:(g + 1) * dv] = o[:, g * tq:(g + 1) * tq].T.astype(o_ref.dtype)


def _flash(q, k, vt, *, n_kv_heads, group, dq, dv, q_col, k_col, v_row, tq, tk):
    b, s, _ = q.shape
    sk = k.shape[1]
    tk = min(tk, sk)
    nq = s // tq
    q_block = (None, tq, group * dq)
    return pl.pallas_call(
        functools.partial(_flash_kernel, group=group, dq=dq, dv=dv, tk=tk),
        grid=(b, n_kv_heads, nq),
        in_specs=[pl.BlockSpec(q_block, lambda bi, h, qi: (bi, qi, q_col + h)),
                  pl.BlockSpec(q_block, lambda bi, h, qi: (bi, jnp.minimum(qi + 1, nq - 1), q_col + h)),
                  pl.BlockSpec((None, sk, dq), lambda bi, h, qi: (bi, 0, k_col + h)),
                  pl.BlockSpec((dv, sk), lambda bi, h, qi: (v_row + h, bi))],
        out_specs=pl.BlockSpec((None, tq, group * dv), lambda bi, h, qi: (bi, qi, h)),
        out_shape=jax.ShapeDtypeStruct((b, s, n_kv_heads * group * dv), BF16),
        scratch_shapes=[pltpu.VMEM((2, tk, group * tq), F32), pltpu.VMEM((1, group * tq), F32)],
        compiler_params=_cparams(("arbitrary", "arbitrary", "arbitrary")),
        name="flash",
    )(q, q, k, vt)


def _merge_kernel(oa_ref, ob_ref, om_ref, g0_ref, g1_ref, g2_ref, x_ref, wa_ref, wb_ref, wm_ref,
                  wo_ref, gn_ref, h_ref, n2_ref):
    merged = g0_ref[...].astype(F32) * _dot(oa_ref[...], wa_ref[...])
    merged = merged + g1_ref[...].astype(F32) * _dot(ob_ref[...], wb_ref[...])
    merged = merged + g2_ref[...].astype(F32) * _dot(om_ref[...], wm_ref[...])
    h = x_ref[...] + _dot(merged.astype(BF16), wo_ref[...])
    h_ref[...] = h
    n2_ref[...] = _rms(h, gn_ref[...]).astype(n2_ref.dtype)


def _merge(oa, ob, om, gates, x, wa, wb, wm, wo, gn, tm=256):
    t, d = x.shape
    const = lambda i: (0, 0)
    row = lambda a: pl.BlockSpec((tm, a.shape[1]), lambda i: (i, 0))
    wspec = lambda a: pl.BlockSpec(a.shape, const, pipeline_mode=pl.Buffered(1))
    gate = lambda br: pl.BlockSpec((tm, d), lambda i: (i, br))
    return pl.pallas_call(
        _merge_kernel,
        grid=(t // tm,),
        in_specs=[row(oa), row(ob), row(om), gate(0), gate(1), gate(2), row(x),
                  wspec(wa), wspec(wb), wspec(wm), wspec(wo), pl.BlockSpec((1, d), const)],
        out_specs=[pl.BlockSpec((tm, d), lambda i: (i, 0)), pl.BlockSpec((tm, d), lambda i: (i, 0))],
        out_shape=[jax.ShapeDtypeStruct((t, d), F32), jax.ShapeDtypeStruct((t, d), BF16)],
        compiler_params=_cparams(("parallel",)),
        name="merge",
    )(oa, ob, om, gates, gates, gates, x, wa, wb, wm, wo, gn)


def _mlp_kernel(n_ref, h_ref, wu_ref, wd_ref, g_ref, y_ref, acc_ref, *, final_norm):
    f = pl.program_id(1)

    @pl.when(f == 0)
    def _():
        acc_ref[...] = jnp.zeros_like(acc_ref)

    u = _dot(n_ref[...], wu_ref[...])
    a = jnp.square(jnp.maximum(u, 0.0)).astype(BF16)
    acc_ref[...] += _dot(a, wd_ref[...])

    @pl.when(f == pl.num_programs(1) - 1)
    def _():
        y = h_ref[...] + acc_ref[...]
        y_ref[...] = _rms(y, g_ref[...]) if final_norm else y


def _mlp(n2, h, wu, wd, g, final_norm, tm=512, tf=1024):
    t, d = h.shape
    dff = wu.shape[1]
    return pl.pallas_call(
        functools.partial(_mlp_kernel, final_norm=final_norm),
        grid=(t // tm, dff // tf),
        in_specs=[pl.BlockSpec((tm, d), lambda i, f: (i, 0)),
                  pl.BlockSpec((tm, d), lambda i, f: (i, 0)),
                  pl.BlockSpec((d, tf), lambda i, f: (0, f)),
                  pl.BlockSpec((tf, d), lambda i, f: (f, 0)),
                  pl.BlockSpec((1, d), lambda i, f: (0, 0))],
        out_specs=pl.BlockSpec((tm, d), lambda i, f: (i, 0)),
        out_shape=jax.ShapeDtypeStruct((t, d), F32),
        scratch_shapes=[pltpu.VMEM((tm, d), F32)],
        compiler_params=_cparams(("parallel", "arbitrary")),
        name="mlp",
    )(n2, h, wu, wd, g)


def _half_split_perm(w, dim):
    q = dim // 4
    r_lo, r_hi, c_lo, c_hi = (w[..., i * q:(i + 1) * q] for i in range(4))
    pad = jnp.zeros(w.shape[:-1] + (LANES // 2 - 2 * q,), w.dtype)
    return jnp.concatenate([r_lo, c_lo, pad, r_hi, c_hi, pad], axis=-1)


def _rope_tables(seq, dim):
    n_rows = seq // GRID_W
    h = dim // 2
    inv_freq = ROPE_THETA ** (-jnp.arange(0, h, 2, dtype=F32) / h)

    def cs(n_pos):
        ang = jnp.arange(n_pos, dtype=jnp.int32).astype(F32)[:, None] * inv_freq[None, :]
        return jnp.cos(ang), jnp.sin(ang)

    def lanes(row_lo, col_lo, row_hi, col_hi):
        zr, zc = jnp.zeros_like(row_lo), jnp.zeros_like(col_lo)
        pad = lambda a: jnp.zeros((a.shape[0], LANES // 2 - dim // 2), F32)
        by_row = jnp.concatenate([row_lo, zr, pad(zr), row_hi, zr, pad(zr)], axis=1)
        by_col = jnp.concatenate([zc, col_lo, pad(zc), zc, col_hi, pad(zc)], axis=1)
        return (by_row[:, None, :] + by_col[None, :, :]).reshape(seq, LANES)

    (cr, sr), (cc, sc) = cs(n_rows), cs(GRID_W)
    return lanes(cr, cc, cr, cc), lanes(-sr, -sc, sr, sc)


def _prep_layer(l, w_in, g_qa, g_ka, w_q_b, w_kv_b, w_mem_kv, w_br_a, w_br_b, w_br_m, w_out, w_up,
                w_down):
    d = w_in.shape[1]
    wi = w_in[l].astype(BF16)
    nq, nk = HEADS_A * HEAD_DIM_A, KV_HEADS_A * HEAD_DIM_A
    lat_w = Q_LORA_B + KV_LORA_B + ROPE_DIM_B
    qm_w = HEADS_M * HEAD_DIM_M
    c_lat = nq + 2 * nk
    c_qm, c_gate = c_lat + lat_w, c_lat + lat_w + qm_w
    c_rope = c_lat + Q_LORA_B + KV_LORA_B

    def perm_heads(w, n_heads):
        return _half_split_perm(w.reshape(d, n_heads, HEAD_DIM_A), HEAD_DIM_A).reshape(d, -1)

    w_a = jnp.concatenate([perm_heads(wi[:, :nq], HEADS_A), perm_heads(wi[:, nq:nq + nk], KV_HEADS_A),
                           wi[:, nq + nk:c_lat], wi[:, c_qm:c_gate]], axis=1)
    g_a = jnp.concatenate(
        [jnp.tile(_half_split_perm(g_qa[l], HEAD_DIM_A) * (HEAD_DIM_A ** -0.5 * LOG2E), HEADS_A),
         jnp.tile(_half_split_perm(g_ka[l], HEAD_DIM_A), KV_HEADS_A)])[None]
    w_lat = jnp.concatenate([wi[:, c_lat:c_rope], _half_split_perm(wi[:, c_rope:c_qm], ROPE_DIM_B)],
                            axis=1)
    w_gate = wi[:, c_gate:]
    wq = w_q_b[l].astype(BF16).reshape(Q_LORA_B, HEADS_B, NOPE_DIM_B + ROPE_DIM_B)
    wq = jnp.concatenate([wq[:, :, :NOPE_DIM_B], _half_split_perm(wq[:, :, NOPE_DIM_B:], ROPE_DIM_B)],
                         axis=-1).reshape(Q_LORA_B, HEADS_B * QB_PAD)
    wkv = w_kv_b[l].astype(BF16).reshape(KV_LORA_B, HEADS_B, NOPE_DIM_B + V_DIM_B)
    wkv = jnp.concatenate([wkv[:, :, :NOPE_DIM_B].reshape(KV_LORA_B, -1),
                           wkv[:, :, NOPE_DIM_B:].reshape(KV_LORA_B, -1)], axis=1)
    return dict(w_a=w_a, g_a=g_a, w_lat=w_lat, w_gate=w_gate, wq=wq, wkv=wkv,
                w_mem=w_mem_kv[l].astype(BF16), w_br_a=w_br_a[l].astype(BF16),
                w_br_b=w_br_b[l].astype(BF16), w_br_m=w_br_m[l].astype(BF16),
                w_out=w_out[l].astype(BF16), w_up=w_up[l].astype(BF16),
                w_down=w_down[l].astype(BF16))


def _trunk(x, mem, layers, gains, g_final):
    b, s, d = x.shape
    t = b * s
    n_mem = mem.shape[1]
    tabs_a = _rope_tables(s, HEAD_DIM_A)
    tabs_b = _rope_tables(s, ROPE_DIM_B)
    h = x.reshape(t, d)
    mem2 = mem.reshape(b * n_mem, d)
    depth = len(layers)
    tk = min(1024, s // 4)
    for l, (w, g) in enumerate(zip(layers, gains)):
        n, qk_a, vt_a, q_m = _qkv_a(h, g["g_mix"], w["w_a"], w["g_a"], tabs_a, s)
        qk_a = qk_a.reshape(b, s, -1)
        q_b, k_b, vt_b = _latent(n, w["w_lat"], g["g_cq"], g["g_ckv"], w["wq"], w["wkv"], tabs_b, s)
        gates = _gates(n, w["w_gate"])
        k_m, vt_m = _mem_kv(mem2, g["g_mem"], w["w_mem"])

        o_a = _flash(qk_a, qk_a, vt_a, n_kv_heads=KV_HEADS_A, group=HEADS_A // KV_HEADS_A,
                     dq=HEAD_DIM_A, dv=HEAD_DIM_A, q_col=0, k_col=HEADS_A, v_row=0, tq=512, tk=tk)
        o_b = _flash(q_b.reshape(b, s, -1), k_b.reshape(b, s, -1), vt_b,
                     n_kv_heads=HEADS_B, group=1, dq=QB_PAD, dv=V_DIM_B, q_col=0, k_col=0, v_row=0,
                     tq=min(2048, s // 2), tk=tk)
        o_m = _flash(q_m.reshape(b, s, -1), k_m.reshape(b, n_mem, -1), vt_m, n_kv_heads=HEADS_M,
                     group=1, dq=HEAD_DIM_M, dv=HEAD_DIM_M, q_col=0, k_col=0, v_row=0, tq=1024,
                     tk=n_mem)

        h, n2 = _merge(o_a.reshape(t, -1), o_b.reshape(t, -1), o_m.reshape(t, -1), gates, h,
                       w["w_br_a"], w["w_br_b"], w["w_br_m"], w["w_out"], g["g_mlp"])
        last = l == depth - 1
        h = _mlp(n2, h, w["w_up"], w["w_down"], g_final if last else g["g_mlp"], final_norm=last)
    return h.reshape(b, s, d)


def kernel(x_prompt, x_sample, mem_prompt, mem_sample, g_mix, w_in, g_qa, g_ka, g_cq, w_q_b, g_ckv,
           w_kv_b, g_mem, w_mem_kv, w_br_a, w_br_b, w_br_m, w_out, g_mlp, w_up, w_down, g_final):
    depth = w_in.shape[0]
    layers = [_prep_layer(l, w_in, g_qa, g_ka, w_q_b, w_kv_b, w_mem_kv, w_br_a, w_br_b, w_br_m,
                          w_out, w_up, w_down) for l in range(depth)]
    gains = [dict(g_mix=g_mix[l][None], g_cq=g_cq[l][None], g_ckv=g_ckv[l][None],
                  g_mem=g_mem[l][None], g_mlp=g_mlp[l][None]) for l in range(depth)]
    gf = g_final[None]
    y_prompt = _trunk(x_prompt, mem_prompt, layers, gains, gf)
    y_sample = _trunk(x_sample, mem_sample, layers, gains, gf)
    return (y_prompt, y_sample)
```

```python
import functools
import math

import jax
import jax.numpy as jnp
from jax import lax
from jax.experimental import pallas as pl
from jax.experimental.pallas import tpu as pltpu

F32 = jnp.float32
BF16 = jnp.bfloat16

EPS = 1e-6
ROPE_THETA = 10000.0
GRID_W = 64
LOG2E = math.log2(math.e)

HEADS_A, KV_HEADS_A, HEAD_DIM_A = 8, 2, 128
HEADS_B, Q_LORA_B, KV_LORA_B = 8, 512, 256
NOPE_DIM_B, ROPE_DIM_B, V_DIM_B = 128, 64, 128
HEADS_M, HEAD_DIM_M = 4, 128
N_BRANCH = 3
LANES = 128
QB_PAD = 256

VMEM_LIMIT = 56 * 1024 * 1024


def _cparams(sem):
    return pltpu.CompilerParams(dimension_semantics=sem, vmem_limit_bytes=VMEM_LIMIT)


def _rms(x, g):
    y = x * lax.rsqrt(jnp.mean(x * x, axis=-1, keepdims=True) + EPS)
    return y * g


def _dot(a, b):
    return jnp.dot(a, b, preferred_element_type=F32)


def _rope(x, cos, sin):
    return x * cos + pltpu.roll(x, LANES // 2, 1) * sin


def _qkv_a_kernel(x_ref, gx_ref, w_ref, g_ref, cos_ref, sin_ref, n_ref, qk_ref, vt_ref, qm_ref, *,
                  qm_scale):
    n = _rms(x_ref[...], gx_ref[...]).astype(n_ref.dtype)
    n_ref[...] = n
    z = _dot(n, w_ref[...])
    cos, sin = cos_ref[...], sin_ref[...]
    n_qk = qk_ref.shape[1]
    n_v = vt_ref.shape[0]
    for h in range(n_qk // LANES):
        c = slice(h * LANES, (h + 1) * LANES)
        qk_ref[:, c] = _rope(_rms(z[:, c], g_ref[:, c]), cos, sin).astype(qk_ref.dtype)
    vt_ref[...] = z[:, n_qk:n_qk + n_v].T.astype(vt_ref.dtype)
    qm_ref[...] = (z[:, n_qk + n_v:] * qm_scale).astype(qm_ref.dtype)


def _qkv_a(x, gx, w, g, tabs, seq, tm=512):
    t, d = x.shape
    n_qk = (HEADS_A + KV_HEADS_A) * HEAD_DIM_A
    n_v = KV_HEADS_A * HEAD_DIM_A
    n_qm = HEADS_M * HEAD_DIM_M
    cos, sin = tabs
    nblk = seq // tm
    const = lambda i: (0, 0)
    tab_spec = pl.BlockSpec((tm, LANES), lambda i: (i % nblk, 0))
    return pl.pallas_call(
        functools.partial(_qkv_a_kernel, qm_scale=HEAD_DIM_M ** -0.5 * LOG2E),
        grid=(t // tm,),
        in_specs=[pl.BlockSpec((tm, d), lambda i: (i, 0)),
                  pl.BlockSpec(gx.shape, const),
                  pl.BlockSpec(w.shape, const, pipeline_mode=pl.Buffered(1)),
                  pl.BlockSpec(g.shape, const), tab_spec, tab_spec],
        out_specs=[pl.BlockSpec((tm, d), lambda i: (i, 0)),
                   pl.BlockSpec((tm, n_qk), lambda i: (i, 0)),
                   pl.BlockSpec((n_v, tm), lambda i: (0, i)),
                   pl.BlockSpec((tm, n_qm), lambda i: (i, 0))],
        out_shape=[jax.ShapeDtypeStruct((t, d), BF16),
                   jax.ShapeDtypeStruct((t, n_qk), BF16),
                   jax.ShapeDtypeStruct((n_v, t), BF16),
                   jax.ShapeDtypeStruct((t, n_qm), BF16)],
        compiler_params=_cparams(("parallel",)),
        name="qkv_a",
    )(x, gx, w, g, cos, sin)


def _latent_kernel(n_ref, wl_ref, gcq_ref, gckv_ref, wq_ref, wkv_ref, cos_ref, sin_ref,
                   q_ref, k_ref, vt_ref, *, q_scale):
    lat = _dot(n_ref[...], wl_ref[...])
    cos, sin = cos_ref[...], sin_ref[...]
    cq = _rms(lat[:, :Q_LORA_B], gcq_ref[...]).astype(BF16)
    qb = _dot(cq, wq_ref[...])
    for h in range(HEADS_B):
        c0 = h * QB_PAD
        q_ref[:, c0:c0 + LANES] = (qb[:, c0:c0 + LANES] * q_scale).astype(q_ref.dtype)
        q_ref[:, c0 + LANES:c0 + QB_PAD] = (
            _rope(qb[:, c0 + LANES:c0 + QB_PAD], cos, sin) * q_scale).astype(q_ref.dtype)
    ckv = _rms(lat[:, Q_LORA_B:Q_LORA_B + KV_LORA_B], gckv_ref[...]).astype(BF16)
    kvb = _dot(ckv, wkv_ref[...])
    kpe = _rope(lat[:, Q_LORA_B + KV_LORA_B:], cos, sin).astype(k_ref.dtype)
    for h in range(HEADS_B):
        c0 = h * QB_PAD
        k_ref[:, c0:c0 + LANES] = kvb[:, h * LANES:(h + 1) * LANES].astype(k_ref.dtype)
        k_ref[:, c0 + LANES:c0 + QB_PAD] = kpe
    vt_ref[...] = kvb[:, HEADS_B * NOPE_DIM_B:].T.astype(vt_ref.dtype)


def _latent(n, wl, gcq, gckv, wq, wkv, tabs, seq, tm=1024):
    t, d = n.shape
    cos, sin = tabs
    nblk = seq // tm
    const = lambda i: (0, 0)
    tab_spec = pl.BlockSpec((tm, LANES), lambda i: (i % nblk, 0))
    row = lambda w: pl.BlockSpec((tm, w), lambda i: (i, 0))
    q_scale = (NOPE_DIM_B + ROPE_DIM_B) ** -0.5 * LOG2E
    return pl.pallas_call(
        functools.partial(_latent_kernel, q_scale=q_scale),
        grid=(t // tm,),
        in_specs=[row(d),
                  pl.BlockSpec(wl.shape, const), pl.BlockSpec(gcq.shape, const),
                  pl.BlockSpec(gckv.shape, const), pl.BlockSpec(wq.shape, const),
                  pl.BlockSpec(wkv.shape, const), tab_spec, tab_spec],
        out_specs=[row(HEADS_B * QB_PAD), row(HEADS_B * QB_PAD),
                   pl.BlockSpec((HEADS_B * V_DIM_B, tm), lambda i: (0, i))],
        out_shape=[jax.ShapeDtypeStruct((t, HEADS_B * QB_PAD), BF16),
                   jax.ShapeDtypeStruct((t, HEADS_B * QB_PAD), BF16),
                   jax.ShapeDtypeStruct((HEADS_B * V_DIM_B, t), BF16)],
        compiler_params=_cparams(("parallel",)),
        name="latent",
    )(n, wl, gcq, gckv, wq, wkv, cos, sin)


def _gate_kernel(n_ref, w_ref, o_ref):
    o_ref[...] = jax.nn.sigmoid(_dot(n_ref[...], w_ref[...])).astype(o_ref.dtype)


def _gates(n, w, tm=1024, tn=2048):
    t, d = n.shape
    nn = w.shape[1]
    return pl.pallas_call(
        _gate_kernel,
        grid=(nn // tn, t // tm),
        in_specs=[pl.BlockSpec((tm, d), lambda j, i: (i, 0)),
                  pl.BlockSpec((d, tn), lambda j, i: (0, j))],
        out_specs=pl.BlockSpec((tm, tn), lambda j, i: (i, j)),
        out_shape=jax.ShapeDtypeStruct((t, nn), BF16),
        compiler_params=_cparams(("arbitrary", "arbitrary")),
        name="gates",
    )(n, w)


def _mem_kv_kernel(m_ref, g_ref, w_ref, k_ref, vt_ref):
    nm = _rms(m_ref[...], g_ref[...]).astype(BF16)
    z = _dot(nm, w_ref[...])
    nk = k_ref.shape[1]
    k_ref[...] = z[:, :nk].astype(k_ref.dtype)
    vt_ref[...] = z[:, nk:].T.astype(vt_ref.dtype)


def _mem_kv(mem, g, w, tm=256):
    t, d = mem.shape
    nn = w.shape[1] // 2
    return pl.pallas_call(
        _mem_kv_kernel,
        grid=(t // tm,),
        in_specs=[pl.BlockSpec((tm, d), lambda i: (i, 0)),
                  pl.BlockSpec((1, d), lambda i: (0, 0)),
                  pl.BlockSpec((d, 2 * nn), lambda i: (0, 0))],
        out_specs=[pl.BlockSpec((tm, nn), lambda i: (i, 0)),
                   pl.BlockSpec((nn, tm), lambda i: (0, i))],
        out_shape=[jax.ShapeDtypeStruct((t, nn), BF16), jax.ShapeDtypeStruct((nn, t), BF16)],
        compiler_params=_cparams(("parallel",)),
        name="mem_kv",
    )(mem, g, w)


def _flash_kernel(q_ref, qn_ref, k_ref, vt_ref, o_ref, s_ref, c_ref, *, group, dq, dv, tk):
    tq = q_ref.shape[0]
    nk = k_ref.shape[0] // tk
    cols = group * tq

    def stacked(ref):
        if group == 1:
            return ref[...]
        return jnp.concatenate([ref[:, g * dq:(g + 1) * dq] for g in range(group)], axis=0)

    q = stacked(q_ref)

    def scores(qs, j):
        ks = k_ref[pl.ds(pl.multiple_of(j * tk, tk), tk), :]
        s = lax.dot_general(ks, qs, (((1,), (1,)), ((), ())), preferred_element_type=F32)
        return s, jnp.max(s, axis=0, keepdims=True)

    def consume(j, s, s_max, m, l, acc):
        vt = vt_ref[:, pl.ds(pl.multiple_of(j * tk, tk), tk)]
        m_new = jnp.maximum(m, s_max)
        alpha = jnp.exp2(m - m_new)
        p = jnp.exp2(s - m_new)
        l_new = alpha * l + jnp.sum(p, axis=0, keepdims=True)
        acc_new = alpha * acc + _dot(vt, p.astype(BF16))
        return m_new, l_new, acc_new

    state = (jnp.full((1, cols), -jnp.inf, F32), jnp.zeros((1, cols), F32),
             jnp.zeros((dv, cols), F32))
    if nk == 1:
        _, l, acc = consume(0, *scores(q, 0), *state)
    else:
        assert nk % 2 == 0

        @pl.when(pl.program_id(2) == 0)
        def _():
            s0, c0 = scores(q, 0)
            s_ref[0] = s0
            c_ref[...] = c0

        def pair(i, carry):
            m, l, acc, c_even = carry
            j = 2 * i
            s_odd, c_odd = scores(q, j + 1)
            s_ref[1] = s_odd
            m, l, acc = consume(j, s_ref[0], c_even, m, l, acc)
            s_even, c_next = scores(q, j + 2)
            s_ref[0] = s_even
            m, l, acc = consume(j + 1, s_ref[1], c_odd, m, l, acc)
            return m, l, acc, c_next

        m, l, acc, c_even = lax.fori_loop(0, nk // 2 - 1, pair, state + (c_ref[...],))
        s_odd, c_odd = scores(q, nk - 1)
        s_ref[1] = s_odd
        m, l, acc = consume(nk - 2, s_ref[0], c_even, m, l, acc)
        s_next, c_next = scores(stacked(qn_ref), 0)
        s_ref[0] = s_next
        c_ref[...] = c_next
        _, l, acc = consume(nk - 1, s_ref[1], c_odd, m, l, acc)
    o = acc / l
    for g in range(group):
        o_ref[:, g * dv:(g + 1) * dv] = o[:, g * tq:(g + 1) * tq].T.astype(o_ref.dtype)


def _flash(q, k, vt, *, n_kv_heads, group, dq, dv, q_col, k_col, v_row, tq, tk):
    b, s, _ = q.shape
    sk = k.shape[1]
    tk = min(tk, sk)
    nq = s // tq
    q_block = (None, tq, group * dq)
    return pl.pallas_call(
        functools.partial(_flash_kernel, group=group, dq=dq, dv=dv, tk=tk),
        grid=(b, n_kv_heads, nq),
        in_specs=[pl.BlockSpec(q_block, lambda bi, h, qi: (bi, qi, q_col + h)),
                  pl.BlockSpec(q_block, lambda bi, h, qi: (bi, jnp.minimum(qi + 1, nq - 1), q_col + h)),
                  pl.BlockSpec((None, sk, dq), lambda bi, h, qi: (bi, 0, k_col + h)),
                  pl.BlockSpec((dv, sk), lambda bi, h, qi: (v_row + h, bi))],
        out_specs=pl.BlockSpec((None, tq, group * dv), lambda bi, h, qi: (bi, qi, h)),
        out_shape=jax.ShapeDtypeStruct((b, s, n_kv_heads * group * dv), BF16),
        scratch_shapes=[pltpu.VMEM((2, tk, group * tq), F32), pltpu.VMEM((1, group * tq), F32)],
        compiler_params=_cparams(("arbitrary", "arbitrary", "arbitrary")),
        name="flash",
    )(q, q, k, vt)


def _merge_kernel(oa_ref, ob_ref, om_ref, g0_ref, g1_ref, g2_ref, x_ref, wa_ref, wb_ref, wm_ref,
                  wo_ref, gn_ref, h_ref, n2_ref):
    merged = g0_ref[...].astype(F32) * _dot(oa_ref[...], wa_ref[...])
    merged = merged + g1_ref[...].astype(F32) * _dot(ob_ref[...], wb_ref[...])
    merged = merged + g2_ref[...].astype(F32) * _dot(om_ref[...], wm_ref[...])
    h = x_ref[...] + _dot(merged.astype(BF16), wo_ref[...])
    h_ref[...] = h
    n2_ref[...] = _rms(h, gn_ref[...]).astype(n2_ref.dtype)


def _merge(oa, ob, om, gates, x, wa, wb, wm, wo, gn, tm=256):
    t, d = x.shape
    const = lambda i: (0, 0)
    row = lambda a: pl.BlockSpec((tm, a.shape[1]), lambda i: (i, 0))
    wspec = lambda a: pl.BlockSpec(a.shape, const, pipeline_mode=pl.Buffered(1))
    gate = lambda br: pl.BlockSpec((tm, d), lambda i: (i, br))
    return pl.pallas_call(
        _merge_kernel,
        grid=(t // tm,),
        in_specs=[row(oa), row(ob), row(om), gate(0), gate(1), gate(2), row(x),
                  wspec(wa), wspec(wb), wspec(wm), wspec(wo), pl.BlockSpec((1, d), const)],
        out_specs=[pl.BlockSpec((tm, d), lambda i: (i, 0)), pl.BlockSpec((tm, d), lambda i: (i, 0))],
        out_shape=[jax.ShapeDtypeStruct((t, d), F32), jax.ShapeDtypeStruct((t, d), BF16)],
        compiler_params=_cparams(("parallel",)),
        name="merge",
    )(oa, ob, om, gates, gates, gates, x, wa, wb, wm, wo, gn)


def _mlp_kernel(n_ref, h_ref, wu_ref, wd_ref, g_ref, y_ref, acc_ref, *, final_norm):
    f = pl.program_id(1)

    @pl.when(f == 0)
    def _():
        acc_ref[...] = jnp.zeros_like(acc_ref)

    u = _dot(n_ref[...], wu_ref[...])
    a = jnp.square(jnp.maximum(u, 0.0)).astype(BF16)
    acc_ref[...] += _dot(a, wd_ref[...])

    @pl.when(f == pl.num_programs(1) - 1)
    def _():
        y = h_ref[...] + acc_ref[...]
        y_ref[...] = _rms(y, g_ref[...]) if final_norm else y


def _mlp(n2, h, wu, wd, g, final_norm, tm=512, tf=1024):
    t, d = h.shape
    dff = wu.shape[1]
    return pl.pallas_call(
        functools.partial(_mlp_kernel, final_norm=final_norm),
        grid=(t // tm, dff // tf),
        in_specs=[pl.BlockSpec((tm, d), lambda i, f: (i, 0)),
                  pl.BlockSpec((tm, d), lambda i, f: (i, 0)),
                  pl.BlockSpec((d, tf), lambda i, f: (0, f)),
                  pl.BlockSpec((tf, d), lambda i, f: (f, 0)),
                  pl.BlockSpec((1, d), lambda i, f: (0, 0))],
        out_specs=pl.BlockSpec((tm, d), lambda i, f: (i, 0)),
        out_shape=jax.ShapeDtypeStruct((t, d), F32),
        scratch_shapes=[pltpu.VMEM((tm, d), F32)],
        compiler_params=_cparams(("parallel", "arbitrary")),
        name="mlp",
    )(n2, h, wu, wd, g)


def _half_split_perm(w, dim):
    q = dim // 4
    r_lo, r_hi, c_lo, c_hi = (w[..., i * q:(i + 1) * q] for i in range(4))
    pad = jnp.zeros(w.shape[:-1] + (LANES // 2 - 2 * q,), w.dtype)
    return jnp.concatenate([r_lo, c_lo, pad, r_hi, c_hi, pad], axis=-1)


def _rope_tables(seq, dim):
    n_rows = seq // GRID_W
    h = dim // 2
    inv_freq = ROPE_THETA ** (-jnp.arange(0, h, 2, dtype=F32) / h)

    def cs(n_pos):
        ang = jnp.arange(n_pos, dtype=jnp.int32).astype(F32)[:, None] * inv_freq[None, :]
        return jnp.cos(ang), jnp.sin(ang)

    def lanes(row_lo, col_lo, row_hi, col_hi):
        zr, zc = jnp.zeros_like(row_lo), jnp.zeros_like(col_lo)
        pad = lambda a: jnp.zeros((a.shape[0], LANES // 2 - dim // 2), F32)
        by_row = jnp.concatenate([row_lo, zr, pad(zr), row_hi, zr, pad(zr)], axis=1)
        by_col = jnp.concatenate([zc, col_lo, pad(zc), zc, col_hi, pad(zc)], axis=1)
        return (by_row[:, None, :] + by_col[None, :, :]).reshape(seq, LANES)

    (cr, sr), (cc, sc) = cs(n_rows), cs(GRID_W)
    return lanes(cr, cc, cr, cc), lanes(-sr, -sc, sr, sc)


def _prep_layer(l, w_in, g_qa, g_ka, w_q_b, w_kv_b, w_mem_kv, w_br_a, w_br_b, w_br_m, w_out, w_up,
                w_down):
    d = w_in.shape[1]
    wi = w_in[l].astype(BF16)
    nq, nk = HEADS_A * HEAD_DIM_A, KV_HEADS_A * HEAD_DIM_A
    lat_w = Q_LORA_B + KV_LORA_B + ROPE_DIM_B
    qm_w = HEADS_M * HEAD_DIM_M
    c_lat = nq + 2 * nk
    c_qm, c_gate = c_lat + lat_w, c_lat + lat_w + qm_w
    c_rope = c_lat + Q_LORA_B + KV_LORA_B

    def perm_heads(w, n_heads):
        return _half_split_perm(w.reshape(d, n_heads, HEAD_DIM_A), HEAD_DIM_A).reshape(d, -1)

    w_a = jnp.concatenate([perm_heads(wi[:, :nq], HEADS_A), perm_heads(wi[:, nq:nq + nk], KV_HEADS_A),
                           wi[:, nq + nk:c_lat], wi[:, c_qm:c_gate]], axis=1)
    g_a = jnp.concatenate(
        [jnp.tile(_half_split_perm(g_qa[l], HEAD_DIM_A) * (HEAD_DIM_A ** -0.5 * LOG2E), HEADS_A),
         jnp.tile(_half_split_perm(g_ka[l], HEAD_DIM_A), KV_HEADS_A)])[None]
    w_lat = jnp.concatenate([wi[:, c_lat:c_rope], _half_split_perm(wi[:, c_rope:c_qm], ROPE_DIM_B)],
                            axis=1)
    w_gate = wi[:, c_gate:]
    wq = w_q_b[l].astype(BF16).reshape(Q_LORA_B, HEADS_B, NOPE_DIM_B + ROPE_DIM_B)
    wq = jnp.concatenate([wq[:, :, :NOPE_DIM_B], _half_split_perm(wq[:, :, NOPE_DIM_B:], ROPE_DIM_B)],
                         axis=-1).reshape(Q_LORA_B, HEADS_B * QB_PAD)
    wkv = w_kv_b[l].astype(BF16).reshape(KV_LORA_B, HEADS_B, NOPE_DIM_B + V_DIM_B)
    wkv = jnp.concatenate([wkv[:, :, :NOPE_DIM_B].reshape(KV_LORA_B, -1),
                           wkv[:, :, NOPE_DIM_B:].reshape(KV_LORA_B, -1)], axis=1)
    return dict(w_a=w_a, g_a=g_a, w_lat=w_lat, w_gate=w_gate, wq=wq, wkv=wkv,
                w_mem=w_mem_kv[l].astype(BF16), w_br_a=w_br_a[l].astype(BF16),
                w_br_b=w_br_b[l].astype(BF16), w_br_m=w_br_m[l].astype(BF16),
                w_out=w_out[l].astype(BF16), w_up=w_up[l].astype(BF16),
                w_down=w_down[l].astype(BF16))


def _trunk(x, mem, layers, gains, g_final):
    b, s, d = x.shape
    t = b * s
    n_mem = mem.shape[1]
    tabs_a = _rope_tables(s, HEAD_DIM_A)
    tabs_b = _rope_tables(s, ROPE_DIM_B)
    h = x.reshape(t, d)
    mem2 = mem.reshape(b * n_mem, d)
    depth = len(layers)
    tk = min(1024, s // 2)
    for l, (w, g) in enumerate(zip(layers, gains)):
        n, qk_a, vt_a, q_m = _qkv_a(h, g["g_mix"], w["w_a"], w["g_a"], tabs_a, s)
        qk_a = qk_a.reshape(b, s, -1)
        q_b, k_b, vt_b = _latent(n, w["w_lat"], g["g_cq"], g["g_ckv"], w["wq"], w["wkv"], tabs_b, s)
        gates = _gates(n, w["w_gate"])
        k_m, vt_m = _mem_kv(mem2, g["g_mem"], w["w_mem"])

        o_a = _flash(qk_a, qk_a, vt_a, n_kv_heads=KV_HEADS_A, group=HEADS_A // KV_HEADS_A,
                     dq=HEAD_DIM_A, dv=HEAD_DIM_A, q_col=0, k_col=HEADS_A, v_row=0, tq=512, tk=tk)
        o_b = _flash(q_b.reshape(b, s, -1), k_b.reshape(b, s, -1), vt_b,
                     n_kv_heads=HEADS_B, group=1, dq=QB_PAD, dv=V_DIM_B, q_col=0, k_col=0, v_row=0,
                     tq=min(2048, s // 2), tk=tk)
        o_m = _flash(q_m.reshape(b, s, -1), k_m.reshape(b, n_mem, -1), vt_m, n_kv_heads=HEADS_M,
                     group=1, dq=HEAD_DIM_M, dv=HEAD_DIM_M, q_col=0, k_col=0, v_row=0, tq=1024,
                     tk=n_mem)

        h, n2 = _merge(o_a.reshape(t, -1), o_b.reshape(t, -1), o_m.reshape(t, -1), gates, h,
                       w["w_br_a"], w["w_br_b"], w["w_br_m"], w["w_out"], g["g_mlp"])
        last = l == depth - 1
        h = _mlp(n2, h, w["w_up"], w["w_down"], g_final if last else g["g_mlp"], final_norm=last)
    return h.reshape(b, s, d)


def kernel(x_prompt, x_sample, mem_prompt, mem_sample, g_mix, w_in, g_qa, g_ka, g_cq, w_q_b, g_ckv,
           w_kv_b, g_mem, w_mem_kv, w_br_a, w_br_b, w_br_m, w_out, g_mlp, w_up, w_down, g_final):
    depth = w_in.shape[0]
    layers = [_prep_layer(l, w_in, g_qa, g_ka, w_q_b, w_kv_b, w_mem_kv, w_br_a, w_br_b, w_br_m,
                          w_out, w_up, w_down) for l in range(depth)]
    gains = [dict(g_mix=g_mix[l][None], g_cq=g_cq[l][None], g_ckv=g_ckv[l][None],
                  g_mem=g_mem[l][None], g_mlp=g_mlp[l][None]) for l in range(depth)]
    gf = g_final[None]
    y_prompt = _trunk(x_prompt, mem_prompt, layers, gains, gf)
    y_sample = _trunk(x_sample, mem_sample, layers, gains, gf)
    return (y_prompt, y_sample)
```

```python
import functools
import math

import jax
import jax.numpy as jnp
from jax import lax
from jax.experimental import pallas as pl
from jax.experimental.pallas import tpu as pltpu

F32 = jnp.float32
BF16 = jnp.bfloat16

EPS = 1e-6
ROPE_THETA = 10000.0
GRID_W = 64
LOG2E = math.log2(math.e)

HEADS_A, KV_HEADS_A, HEAD_DIM_A = 8, 2, 128
HEADS_B, Q_LORA_B, KV_LORA_B = 8, 512, 256
NOPE_DIM_B, ROPE_DIM_B, V_DIM_B = 128, 64, 128
HEADS_M, HEAD_DIM_M = 4, 128
N_BRANCH = 3
LANES = 128
QB_PAD = 256

VMEM_LIMIT = 56 * 1024 * 1024


def _cparams(sem):
    return pltpu.CompilerParams(dimension_semantics=sem, vmem_limit_bytes=VMEM_LIMIT)


def _rms(x, g):
    y = x * lax.rsqrt(jnp.mean(x * x, axis=-1, keepdims=True) + EPS)
    return y * g


def _dot(a, b):
    return jnp.dot(a, b, preferred_element_type=F32)


def _rope(x, cos, sin):
    return x * cos + pltpu.roll(x, LANES // 2, 1) * sin


def _qkv_a_kernel(x_ref, gx_ref, w_ref, g_ref, cos_ref, sin_ref, n_ref, qk_ref, vt_ref, qm_ref, *,
                  qm_scale):
    n = _rms(x_ref[...], gx_ref[...]).astype(n_ref.dtype)
    n_ref[...] = n
    z = _dot(n, w_ref[...])
    cos, sin = cos_ref[...], sin_ref[...]
    n_qk = qk_ref.shape[1]
    n_v = vt_ref.shape[0]
    for h in range(n_qk // LANES):
        c = slice(h * LANES, (h + 1) * LANES)
        qk_ref[:, c] = _rope(_rms(z[:, c], g_ref[:, c]), cos, sin).astype(qk_ref.dtype)
    vt_ref[...] = z[:, n_qk:n_qk + n_v].T.astype(vt_ref.dtype)
    qm_ref[...] = (z[:, n_qk + n_v:] * qm_scale).astype(qm_ref.dtype)


def _qkv_a(x, gx, w, g, tabs, seq, tm=512):
    t, d = x.shape
    n_qk = (HEADS_A + KV_HEADS_A) * HEAD_DIM_A
    n_v = KV_HEADS_A * HEAD_DIM_A
    n_qm = HEADS_M * HEAD_DIM_M
    cos, sin = tabs
    nblk = seq // tm
    const = lambda i: (0, 0)
    tab_spec = pl.BlockSpec((tm, LANES), lambda i: (i % nblk, 0))
    return pl.pallas_call(
        functools.partial(_qkv_a_kernel, qm_scale=HEAD_DIM_M ** -0.5 * LOG2E),
        grid=(t // tm,),
        in_specs=[pl.BlockSpec((tm, d), lambda i: (i, 0)),
                  pl.BlockSpec(gx.shape, const),
                  pl.BlockSpec(w.shape, const, pipeline_mode=pl.Buffered(1)),
                  pl.BlockSpec(g.shape, const), tab_spec, tab_spec],
        out_specs=[pl.BlockSpec((tm, d), lambda i: (i, 0)),
                   pl.BlockSpec((tm, n_qk), lambda i: (i, 0)),
                   pl.BlockSpec((n_v, tm), lambda i: (0, i)),
                   pl.BlockSpec((tm, n_qm), lambda i: (i, 0))],
        out_shape=[jax.ShapeDtypeStruct((t, d), BF16),
                   jax.ShapeDtypeStruct((t, n_qk), BF16),
                   jax.ShapeDtypeStruct((n_v, t), BF16),
                   jax.ShapeDtypeStruct((t, n_qm), BF16)],
        compiler_params=_cparams(("parallel",)),
        name="qkv_a",
    )(x, gx, w, g, cos, sin)


def _latent_kernel(n_ref, wl_ref, gcq_ref, gckv_ref, wq_ref, wkv_ref, cos_ref, sin_ref,
                   q_ref, k_ref, vt_ref, *, q_scale):
    lat = _dot(n_ref[...], wl_ref[...])
    cos, sin = cos_ref[...], sin_ref[...]
    cq = _rms(lat[:, :Q_LORA_B], gcq_ref[...]).astype(BF16)
    qb = _dot(cq, wq_ref[...])
    for h in range(HEADS_B):
        c0 = h * QB_PAD
        q_ref[:, c0:c0 + LANES] = (qb[:, c0:c0 + LANES] * q_scale).astype(q_ref.dtype)
        q_ref[:, c0 + LANES:c0 + QB_PAD] = (
            _rope(qb[:, c0 + LANES:c0 + QB_PAD], cos, sin) * q_scale).astype(q_ref.dtype)
    ckv = _rms(lat[:, Q_LORA_B:Q_LORA_B + KV_LORA_B], gckv_ref[...]).astype(BF16)
    kvb = _dot(ckv, wkv_ref[...])
    kpe = _rope(lat[:, Q_LORA_B + KV_LORA_B:], cos, sin).astype(k_ref.dtype)
    for h in range(HEADS_B):
        c0 = h * QB_PAD
        k_ref[:, c0:c0 + LANES] = kvb[:, h * LANES:(h + 1) * LANES].astype(k_ref.dtype)
        k_ref[:, c0 + LANES:c0 + QB_PAD] = kpe
    vt_ref[...] = kvb[:, HEADS_B * NOPE_DIM_B:].T.astype(vt_ref.dtype)


def _latent(n, wl, gcq, gckv, wq, wkv, tabs, seq, tm=1024):
    t, d = n.shape
    cos, sin = tabs
    nblk = seq // tm
    const = lambda i: (0, 0)
    tab_spec = pl.BlockSpec((tm, LANES), lambda i: (i % nblk, 0))
    row = lambda w: pl.BlockSpec((tm, w), lambda i: (i, 0))
    q_scale = (NOPE_DIM_B + ROPE_DIM_B) ** -0.5 * LOG2E
    return pl.pallas_call(
        functools.partial(_latent_kernel, q_scale=q_scale),
        grid=(t // tm,),
        in_specs=[row(d),
                  pl.BlockSpec(wl.shape, const), pl.BlockSpec(gcq.shape, const),
                  pl.BlockSpec(gckv.shape, const), pl.BlockSpec(wq.shape, const),
                  pl.BlockSpec(wkv.shape, const), tab_spec, tab_spec],
        out_specs=[row(HEADS_B * QB_PAD), row(HEADS_B * QB_PAD),
                   pl.BlockSpec((HEADS_B * V_DIM_B, tm), lambda i: (0, i))],
        out_shape=[jax.ShapeDtypeStruct((t, HEADS_B * QB_PAD), BF16),
                   jax.ShapeDtypeStruct((t, HEADS_B * QB_PAD), BF16),
                   jax.ShapeDtypeStruct((HEADS_B * V_DIM_B, t), BF16)],
        compiler_params=_cparams(("parallel",)),
        name="latent",
    )(n, wl, gcq, gckv, wq, wkv, cos, sin)


def _gate_kernel(n_ref, w_ref, o_ref):
    o_ref[...] = jax.nn.sigmoid(_dot(n_ref[...], w_ref[...])).astype(o_ref.dtype)


def _gates(n, w, tm=1024, tn=2048):
    t, d = n.shape
    nn = w.shape[1]
    return pl.pallas_call(
        _gate_kernel,
        grid=(nn // tn, t // tm),
        in_specs=[pl.BlockSpec((tm, d), lambda j, i: (i, 0)),
                  pl.BlockSpec((d, tn), lambda j, i: (0, j))],
        out_specs=pl.BlockSpec((tm, tn), lambda j, i: (i, j)),
        out_shape=jax.ShapeDtypeStruct((t, nn), BF16),
        compiler_params=_cparams(("arbitrary", "arbitrary")),
        name="gates",
    )(n, w)


def _mem_kv_kernel(m_ref, g_ref, w_ref, k_ref, vt_ref):
    nm = _rms(m_ref[...], g_ref[...]).astype(BF16)
    z = _dot(nm, w_ref[...])
    nk = k_ref.shape[1]
    k_ref[...] = z[:, :nk].astype(k_ref.dtype)
    vt_ref[...] = z[:, nk:].T.astype(vt_ref.dtype)


def _mem_kv(mem, g, w, tm=256):
    t, d = mem.shape
    nn = w.shape[1] // 2
    return pl.pallas_call(
        _mem_kv_kernel,
        grid=(t // tm,),
        in_specs=[pl.BlockSpec((tm, d), lambda i: (i, 0)),
                  pl.BlockSpec((1, d), lambda i: (0, 0)),
                  pl.BlockSpec((d, 2 * nn), lambda i: (0, 0))],
        out_specs=[pl.BlockSpec((tm, nn), lambda i: (i, 0)),
                   pl.BlockSpec((nn, tm), lambda i: (0, i))],
        out_shape=[jax.ShapeDtypeStruct((t, nn), BF16), jax.ShapeDtypeStruct((nn, t), BF16)],
        compiler_params=_cparams(("parallel",)),
        name="mem_kv",
    )(mem, g, w)


def _flash_kernel(q_ref, qn_ref, k_ref, vt_ref, o_ref, s_ref, c_ref, *, group, dq, dv, tk):
    tq = q_ref.shape[0]
    nk = k_ref.shape[0] // tk
    cols = group * tq

    def stacked(ref):
        if group == 1:
            return ref[...]
        return jnp.concatenate([ref[:, g * dq:(g + 1) * dq] for g in range(group)], axis=0)

    q = stacked(q_ref)

    def scores(qs, j):
        ks = k_ref[pl.ds(pl.multiple_of(j * tk, tk), tk), :]
        s = lax.dot_general(ks, qs, (((1,), (1,)), ((), ())), preferred_element_type=F32)
        return s, jnp.max(s, axis=0, keepdims=True)

    def consume(j, s, s_max, m, l, acc):
        vt = vt_ref[:, pl.ds(pl.multiple_of(j * tk, tk), tk)]
        m_new = jnp.maximum(m, s_max)
        alpha = jnp.exp2(m - m_new)
        p = jnp.exp2(s - m_new)
        l_new = alpha * l + jnp.sum(p, axis=0, keepdims=True)
        acc_new = alpha * acc + _dot(vt, p.astype(BF16))
        return m_new, l_new, acc_new

    state = (jnp.full((1, cols), -jnp.inf, F32), jnp.zeros((1, cols), F32),
             jnp.zeros((dv, cols), F32))
    if nk == 1:
        _, l, acc = consume(0, *scores(q, 0), *state)
    else:
        assert nk % 2 == 0

        @pl.when(pl.program_id(2) == 0)
        def _():
            s0, c0 = scores(q, 0)
            s_ref[0] = s0
            c_ref[...] = c0

        def pair(i, carry):
            m, l, acc, c_even = carry
            j = 2 * i
            s_odd, c_odd = scores(q, j + 1)
            s_ref[1] = s_odd
            m, l, acc = consume(j, s_ref[0], c_even, m, l, acc)
            s_even, c_next = scores(q, j + 2)
            s_ref[0] = s_even
            m, l, acc = consume(j + 1, s_ref[1], c_odd, m, l, acc)
            return m, l, acc, c_next

        m, l, acc, c_even = lax.fori_loop(0, nk // 2 - 1, pair, state + (c_ref[...],))
        s_odd, c_odd = scores(q, nk - 1)
        s_ref[1] = s_odd
        m, l, acc = consume(nk - 2, s_ref[0], c_even, m, l, acc)
        s_next, c_next = scores(stacked(qn_ref), 0)
        s_ref[0] = s_next
        c_ref[...] = c_next
        _, l, acc = consume(nk - 1, s_ref[1], c_odd, m, l, acc)
    o = acc / l
    for g in range(group):
        o_ref[:, g * dv:(g + 1) * dv] = o[:, g * tq:(g + 1) * tq].T.astype(o_ref.dtype)


def _flash(q, k, vt, *, n_kv_heads, group, dq, dv, q_col, k_col, v_row, tq, tk):
    b, s, _ = q.shape
    sk = k.shape[1]
    tk = min(tk, sk)
    nq = s // tq
    q_block = (None, tq, group * dq)
    return pl.pallas_call(
        functools.partial(_flash_kernel, group=group, dq=dq, dv=dv, tk=tk),
        grid=(b, n_kv_heads, nq),
        in_specs=[pl.BlockSpec(q_block, lambda bi, h, qi: (bi, qi, q_col + h)),
                  pl.BlockSpec(q_block, lambda bi, h, qi: (bi, jnp.minimum(qi + 1, nq - 1), q_col + h)),
                  pl.BlockSpec((None, sk, dq), lambda bi, h, qi: (bi, 0, k_col + h)),
                  pl.BlockSpec((dv, sk), lambda bi, h, qi: (v_row + h, bi))],
        out_specs=pl.BlockSpec((None, tq, group * dv), lambda bi, h, qi: (bi, qi, h)),
        out_shape=jax.ShapeDtypeStruct((b, s, n_kv_heads * group * dv), BF16),
        scratch_shapes=[pltpu.VMEM((2, tk, group * tq), F32), pltpu.VMEM((1, group * tq), F32)],
        compiler_params=_cparams(("arbitrary", "arbitrary", "arbitrary")),
        name="flash",
    )(q, q, k, vt)


def _merge_kernel(oa_ref, ob_ref, om_ref, g0_ref, g1_ref, g2_ref, x_ref, wa_ref, wb_ref, wm_ref,
                  wo_ref, gn_ref, h_ref, n2_ref):
    merged = g0_ref[...].astype(F32) * _dot(oa_ref[...], wa_ref[...])
    merged = merged + g1_ref[...].astype(F32) * _dot(ob_ref[...], wb_ref[...])
    merged = merged + g2_ref[...].astype(F32) * _dot(om_ref[...], wm_ref[...])
    h = x_ref[...] + _dot(merged.astype(BF16), wo_ref[...])
    h_ref[...] = h
    n2_ref[...] = _rms(h, gn_ref[...]).astype(n2_ref.dtype)


def _merge(oa, ob, om, gates, x, wa, wb, wm, wo, gn, tm=256):
    t, d = x.shape
    const = lambda i: (0, 0)
    row = lambda a: pl.BlockSpec((tm, a.shape[1]), lambda i: (i, 0))
    wspec = lambda a: pl.BlockSpec(a.shape, const, pipeline_mode=pl.Buffered(1))
    gate = lambda br: pl.BlockSpec((tm, d), lambda i: (i, br))
    return pl.pallas_call(
        _merge_kernel,
        grid=(t // tm,),
        in_specs=[row(oa), row(ob), row(om), gate(0), gate(1), gate(2), row(x),
                  wspec(wa), wspec(wb), wspec(wm), wspec(wo), pl.BlockSpec((1, d), const)],
        out_specs=[pl.BlockSpec((tm, d), lambda i: (i, 0)), pl.BlockSpec((tm, d), lambda i: (i, 0))],
        out_shape=[jax.ShapeDtypeStruct((t, d), F32), jax.ShapeDtypeStruct((t, d), BF16)],
        compiler_params=_cparams(("parallel",)),
        name="merge",
    )(oa, ob, om, gates, gates, gates, x, wa, wb, wm, wo, gn)


def _mlp_kernel(n_ref, h_ref, wu_ref, wd_ref, g_ref, y_ref, acc_ref, *, final_norm):
    f = pl.program_id(1)

    @pl.when(f == 0)
    def _():
        acc_ref[...] = jnp.zeros_like(acc_ref)

    u = _dot(n_ref[...], wu_ref[...])
    a = jnp.square(jnp.maximum(u, 0.0)).astype(BF16)
    acc_ref[...] += _dot(a, wd_ref[...])

    @pl.when(f == pl.num_programs(1) - 1)
    def _():
        y = h_ref[...] + acc_ref[...]
        y_ref[...] = _rms(y, g_ref[...]) if final_norm else y


def _mlp(n2, h, wu, wd, g, final_norm, tm=512, tf=1024):
    t, d = h.shape
    dff = wu.shape[1]
    return pl.pallas_call(
        functools.partial(_mlp_kernel, final_norm=final_norm),
        grid=(t // tm, dff // tf),
        in_specs=[pl.BlockSpec((tm, d), lambda i, f: (i, 0)),
                  pl.BlockSpec((tm, d), lambda i, f: (i, 0)),
                  pl.BlockSpec((d, tf), lambda i, f: (0, f)),
                  pl.BlockSpec((tf, d), lambda i, f: (f, 0)),
                  pl.BlockSpec((1, d), lambda i, f: (0, 0))],
        out_specs=pl.BlockSpec((tm, d), lambda i, f: (i, 0)),
        out_shape=jax.ShapeDtypeStruct((t, d), F32),
        scratch_shapes=[pltpu.VMEM((tm, d), F32)],
        compiler_params=_cparams(("parallel", "arbitrary")),
        name="mlp",
    )(n2, h, wu, wd, g)


def _half_split_perm(w, dim):
    q = dim // 4
    r_lo, r_hi, c_lo, c_hi = (w[..., i * q:(i + 1) * q] for i in range(4))
    pad = jnp.zeros(w.shape[:-1] + (LANES // 2 - 2 * q,), w.dtype)
    return jnp.concatenate([r_lo, c_lo, pad, r_hi, c_hi, pad], axis=-1)


def _rope_tables(seq, dim):
    n_rows = seq // GRID_W
    h = dim // 2
    inv_freq = ROPE_THETA ** (-jnp.arange(0, h, 2, dtype=F32) / h)

    def cs(n_pos):
        ang = jnp.arange(n_pos, dtype=jnp.int32).astype(F32)[:, None] * inv_freq[None, :]
        return jnp.cos(ang), jnp.sin(ang)

    def lanes(row_lo, col_lo, row_hi, col_hi):
        zr, zc = jnp.zeros_like(row_lo), jnp.zeros_like(col_lo)
        pad = lambda a: jnp.zeros((a.shape[0], LANES // 2 - dim // 2), F32)
        by_row = jnp.concatenate([row_lo, zr, pad(zr), row_hi, zr, pad(zr)], axis=1)
        by_col = jnp.concatenate([zc, col_lo, pad(zc), zc, col_hi, pad(zc)], axis=1)
        return (by_row[:, None, :] + by_col[None, :, :]).reshape(seq, LANES)

    (cr, sr), (cc, sc) = cs(n_rows), cs(GRID_W)
    return lanes(cr, cc, cr, cc), lanes(-sr, -sc, sr, sc)


def _prep_layer(l, w_in, g_qa, g_ka, w_q_b, w_kv_b, w_mem_kv, w_br_a, w_br_b, w_br_m, w_out, w_up,
                w_down):
    d = w_in.shape[1]
    wi = w_in[l].astype(BF16)
    nq, nk = HEADS_A * HEAD_DIM_A, KV_HEADS_A * HEAD_DIM_A
    lat_w = Q_LORA_B + KV_LORA_B + ROPE_DIM_B
    qm_w = HEADS_M * HEAD_DIM_M
    c_lat = nq + 2 * nk
    c_qm, c_gate = c_lat + lat_w, c_lat + lat_w + qm_w
    c_rope = c_lat + Q_LORA_B + KV_LORA_B

    def perm_heads(w, n_heads):
        return _half_split_perm(w.reshape(d, n_heads, HEAD_DIM_A), HEAD_DIM_A).reshape(d, -1)

    w_a = jnp.concatenate([perm_heads(wi[:, :nq], HEADS_A), perm_heads(wi[:, nq:nq + nk], KV_HEADS_A),
                           wi[:, nq + nk:c_lat], wi[:, c_qm:c_gate]], axis=1)
    g_a = jnp.concatenate(
        [jnp.tile(_half_split_perm(g_qa[l], HEAD_DIM_A) * (HEAD_DIM_A ** -0.5 * LOG2E), HEADS_A),
         jnp.tile(_half_split_perm(g_ka[l], HEAD_DIM_A), KV_HEADS_A)])[None]
    w_lat = jnp.concatenate([wi[:, c_lat:c_rope], _half_split_perm(wi[:, c_rope:c_qm], ROPE_DIM_B)],
                            axis=1)
    w_gate = wi[:, c_gate:]
    wq = w_q_b[l].astype(BF16).reshape(Q_LORA_B, HEADS_B, NOPE_DIM_B + ROPE_DIM_B)
    wq = jnp.concatenate([wq[:, :, :NOPE_DIM_B], _half_split_perm(wq[:, :, NOPE_DIM_B:], ROPE_DIM_B)],
                         axis=-1).reshape(Q_LORA_B, HEADS_B * QB_PAD)
    wkv = w_kv_b[l].astype(BF16).reshape(KV_LORA_B, HEADS_B, NOPE_DIM_B + V_DIM_B)
    wkv = jnp.concatenate([wkv[:, :, :NOPE_DIM_B].reshape(KV_LORA_B, -1),
                           wkv[:, :, NOPE_DIM_B:].reshape(KV_LORA_B, -1)], axis=1)
    return dict(w_a=w_a, g_a=g_a, w_lat=w_lat, w_gate=w_gate, wq=wq, wkv=wkv,
                w_mem=w_mem_kv[l].astype(BF16), w_br_a=w_br_a[l].astype(BF16),
                w_br_b=w_br_b[l].astype(BF16), w_br_m=w_br_m[l].astype(BF16),
                w_out=w_out[l].astype(BF16), w_up=w_up[l].astype(BF16),
                w_down=w_down[l].astype(BF16))


def _trunk(x, mem, layers, gains, g_final):
    b, s, d = x.shape
    t = b * s
    n_mem = mem.shape[1]
    tabs_a = _rope_tables(s, HEAD_DIM_A)
    tabs_b = _rope_tables(s, ROPE_DIM_B)
    h = x.reshape(t, d)
    mem2 = mem.reshape(b * n_mem, d)
    depth = len(layers)
    tk = min(1024, s // 2)
    for l, (w, g) in enumerate(zip(layers, gains)):
        n, qk_a, vt_a, q_m = _qkv_a(h, g["g_mix"], w["w_a"], w["g_a"], tabs_a, s)
        qk_a = qk_a.reshape(b, s, -1)
        q_b, k_b, vt_b = _latent(n, w["w_lat"], g["g_cq"], g["g_ckv"], w["wq"], w["wkv"], tabs_b, s)
        gates = _gates(n, w["w_gate"])
        k_m, vt_m = _mem_kv(mem2, g["g_mem"], w["w_mem"])

        o_a = _flash(qk_a, qk_a, vt_a, n_kv_heads=KV_HEADS_A, group=HEADS_A // KV_HEADS_A,
                     dq=HEAD_DIM_A, dv=HEAD_DIM_A, q_col=0, k_col=HEADS_A, v_row=0, tq=512, tk=tk)
        o_b = _flash(q_b.reshape(b, s, -1), k_b.reshape(b, s, -1), vt_b,
                     n_kv_heads=HEADS_B, group=1, dq=QB_PAD, dv=V_DIM_B, q_col=0, k_col=0, v_row=0,
                     tq=min(2048, s // 2), tk=tk)
        o_m = _flash(q_m.reshape(b, s, -1), k_m.reshape(b, n_mem, -1), vt_m, n_kv_heads=HEADS_M,
                     group=1, dq=HEAD_DIM_M, dv=HEAD_DIM_M, q_col=0, k_col=0, v_row=0, tq=2048,
                     tk=n_mem)

        h, n2 = _merge(o_a.reshape(t, -1), o_b.reshape(t, -1), o_m.reshape(t, -1), gates, h,
                       w["w_br_a"], w["w_br_b"], w["w_br_m"], w["w_out"], g["g_mlp"])
        last = l == depth - 1
        h = _mlp(n2, h, w["w_up"], w["w_down"], g_final if last else g["g_mlp"], final_norm=last)
    return h.reshape(b, s, d)


def kernel(x_prompt, x_sample, mem_prompt, mem_sample, g_mix, w_in, g_qa, g_ka, g_cq, w_q_b, g_ckv,
           w_kv_b, g_mem, w_mem_kv, w_br_a, w_br_b, w_br_m, w_out, g_mlp, w_up, w_down, g_final):
    depth = w_in.shape[0]
    layers = [_prep_layer(l, w_in, g_qa, g_ka, w_q_b, w_kv_b, w_mem_kv, w_br_a, w_br_b, w_br_m,
                          w_out, w_up, w_down) for l in range(depth)]
    gains = [dict(g_mix=g_mix[l][None], g_cq=g_cq[l][None], g_ckv=g_ckv[l][None],
                  g_mem=g_mem[l][None], g_mlp=g_mlp[l][None]) for l in range(depth)]
    gf = g_final[None]
    y_prompt = _trunk(x_prompt, mem_prompt, layers, gains, gf)
    y_sample = _trunk(x_sample, mem_sample, layers, gains, gf)
    return (y_prompt, y_sample)
```

```python
import functools
import math

import jax
import jax.numpy as jnp
from jax import lax
from jax.experimental import pallas as pl
from jax.experimental.pallas import tpu as pltpu

F32 = jnp.float32
BF16 = jnp.bfloat16

EPS = 1e-6
ROPE_THETA = 10000.0
GRID_W = 64
LOG2E = math.log2(math.e)

HEADS_A, KV_HEADS_A, HEAD_DIM_A = 8, 2, 128
HEADS_B, Q_LORA_B, KV_LORA_B = 8, 512, 256
NOPE_DIM_B, ROPE_DIM_B, V_DIM_B = 128, 64, 128
HEADS_M, HEAD_DIM_M = 4, 128
N_BRANCH = 3
LANES = 128
QB_PAD = 256

VMEM_LIMIT = 56 * 1024 * 1024


def _cparams(sem):
    return pltpu.CompilerParams(dimension_semantics=sem, vmem_limit_bytes=VMEM_LIMIT)


def _rms(x, g):
    y = x * lax.rsqrt(jnp.mean(x * x, axis=-1, keepdims=True) + EPS)
    return y * g


def _dot(a, b):
    return jnp.dot(a, b, preferred_element_type=F32)


def _rope(x, cos, sin):
    return x * cos + pltpu.roll(x, LANES // 2, 1) * sin


def _qkv_a_kernel(x_ref, gx_ref, w_ref, g_ref, cos_ref, sin_ref, n_ref, qk_ref, vt_ref, qm_ref, *,
                  qm_scale):
    n = _rms(x_ref[...], gx_ref[...]).astype(n_ref.dtype)
    n_ref[...] = n
    z = _dot(n, w_ref[...])
    cos, sin = cos_ref[...], sin_ref[...]
    n_qk = qk_ref.shape[1]
    n_v = vt_ref.shape[0]
    for h in range(n_qk // LANES):
        c = slice(h * LANES, (h + 1) * LANES)
        qk_ref[:, c] = _rope(_rms(z[:, c], g_ref[:, c]), cos, sin).astype(qk_ref.dtype)
    vt_ref[...] = z[:, n_qk:n_qk + n_v].T.astype(vt_ref.dtype)
    qm_ref[...] = (z[:, n_qk + n_v:] * qm_scale).astype(qm_ref.dtype)


def _qkv_a(x, gx, w, g, tabs, seq, tm=512):
    t, d = x.shape
    n_qk = (HEADS_A + KV_HEADS_A) * HEAD_DIM_A
    n_v = KV_HEADS_A * HEAD_DIM_A
    n_qm = HEADS_M * HEAD_DIM_M
    cos, sin = tabs
    nblk = seq // tm
    const = lambda i: (0, 0)
    tab_spec = pl.BlockSpec((tm, LANES), lambda i: (i % nblk, 0))
    return pl.pallas_call(
        functools.partial(_qkv_a_kernel, qm_scale=HEAD_DIM_M ** -0.5 * LOG2E),
        grid=(t // tm,),
        in_specs=[pl.BlockSpec((tm, d), lambda i: (i, 0)),
                  pl.BlockSpec(gx.shape, const),
                  pl.BlockSpec(w.shape, const, pipeline_mode=pl.Buffered(1)),
                  pl.BlockSpec(g.shape, const), tab_spec, tab_spec],
        out_specs=[pl.BlockSpec((tm, d), lambda i: (i, 0)),
                   pl.BlockSpec((tm, n_qk), lambda i: (i, 0)),
                   pl.BlockSpec((n_v, tm), lambda i: (0, i)),
                   pl.BlockSpec((tm, n_qm), lambda i: (i, 0))],
        out_shape=[jax.ShapeDtypeStruct((t, d), BF16),
                   jax.ShapeDtypeStruct((t, n_qk), BF16),
                   jax.ShapeDtypeStruct((n_v, t), BF16),
                   jax.ShapeDtypeStruct((t, n_qm), BF16)],
        compiler_params=_cparams(("parallel",)),
        name="qkv_a",
    )(x, gx, w, g, cos, sin)


def _latent_kernel(n_ref, wl_ref, gcq_ref, gckv_ref, wq_ref, wkv_ref, cos_ref, sin_ref,
                   q_ref, k_ref, vt_ref, *, q_scale):
    lat = _dot(n_ref[...], wl_ref[...])
    cos, sin = cos_ref[...], sin_ref[...]
    cq = _rms(lat[:, :Q_LORA_B], gcq_ref[...]).astype(BF16)
    qb = _dot(cq, wq_ref[...])
    for h in range(HEADS_B):
        c0 = h * QB_PAD
        q_ref[:, c0:c0 + LANES] = (qb[:, c0:c0 + LANES] * q_scale).astype(q_ref.dtype)
        q_ref[:, c0 + LANES:c0 + QB_PAD] = (
            _rope(qb[:, c0 + LANES:c0 + QB_PAD], cos, sin) * q_scale).astype(q_ref.dtype)
    ckv = _rms(lat[:, Q_LORA_B:Q_LORA_B + KV_LORA_B], gckv_ref[...]).astype(BF16)
    kvb = _dot(ckv, wkv_ref[...])
    kpe = _rope(lat[:, Q_LORA_B + KV_LORA_B:], cos, sin).astype(k_ref.dtype)
    for h in range(HEADS_B):
        c0 = h * QB_PAD
        k_ref[:, c0:c0 + LANES] = kvb[:, h * LANES:(h + 1) * LANES].astype(k_ref.dtype)
        k_ref[:, c0 + LANES:c0 + QB_PAD] = kpe
    vt_ref[...] = kvb[:, HEADS_B * NOPE_DIM_B:].T.astype(vt_ref.dtype)


def _latent(n, wl, gcq, gckv, wq, wkv, tabs, seq, tm=1024):
    t, d = n.shape
    cos, sin = tabs
    nblk = seq // tm
    const = lambda i: (0, 0)
    tab_spec = pl.BlockSpec((tm, LANES), lambda i: (i % nblk, 0))
    row = lambda w: pl.BlockSpec((tm, w), lambda i: (i, 0))
    q_scale = (NOPE_DIM_B + ROPE_DIM_B) ** -0.5 * LOG2E
    return pl.pallas_call(
        functools.partial(_latent_kernel, q_scale=q_scale),
        grid=(t // tm,),
        in_specs=[row(d),
                  pl.BlockSpec(wl.shape, const), pl.BlockSpec(gcq.shape, const),
                  pl.BlockSpec(gckv.shape, const), pl.BlockSpec(wq.shape, const),
                  pl.BlockSpec(wkv.shape, const), tab_spec, tab_spec],
        out_specs=[row(HEADS_B * QB_PAD), row(HEADS_B * QB_PAD),
                   pl.BlockSpec((HEADS_B * V_DIM_B, tm), lambda i: (0, i))],
        out_shape=[jax.ShapeDtypeStruct((t, HEADS_B * QB_PAD), BF16),
                   jax.ShapeDtypeStruct((t, HEADS_B * QB_PAD), BF16),
                   jax.ShapeDtypeStruct((HEADS_B * V_DIM_B, t), BF16)],
        compiler_params=_cparams(("parallel",)),
        name="latent",
    )(n, wl, gcq, gckv, wq, wkv, cos, sin)


def _gate_kernel(n_ref, w_ref, o_ref):
    o_ref[...] = jax.nn.sigmoid(_dot(n_ref[...], w_ref[...])).astype(o_ref.dtype)


def _gates(n, w, tm=1024, tn=2048):
    t, d = n.shape
    nn = w.shape[1]
    return pl.pallas_call(
        _gate_kernel,
        grid=(nn // tn, t // tm),
        in_specs=[pl.BlockSpec((tm, d), lambda j, i: (i, 0)),
                  pl.BlockSpec((d, tn), lambda j, i: (0, j))],
        out_specs=pl.BlockSpec((tm, tn), lambda j, i: (i, j)),
        out_shape=jax.ShapeDtypeStruct((t, nn), BF16),
        compiler_params=_cparams(("arbitrary", "arbitrary")),
        name="gates",
    )(n, w)


def _mem_kv_kernel(m_ref, g_ref, w_ref, k_ref, vt_ref):
    nm = _rms(m_ref[...], g_ref[...]).astype(BF16)
    z = _dot(nm, w_ref[...])
    nk = k_ref.shape[1]
    k_ref[...] = z[:, :nk].astype(k_ref.dtype)
    vt_ref[...] = z[:, nk:].T.astype(vt_ref.dtype)


def _mem_kv(mem, g, w, tm=256):
    t, d = mem.shape
    nn = w.shape[1] // 2
    return pl.pallas_call(
        _mem_kv_kernel,
        grid=(t // tm,),
        in_specs=[pl.BlockSpec((tm, d), lambda i: (i, 0)),
                  pl.BlockSpec((1, d), lambda i: (0, 0)),
                  pl.BlockSpec((d, 2 * nn), lambda i: (0, 0))],
        out_specs=[pl.BlockSpec((tm, nn), lambda i: (i, 0)),
                   pl.BlockSpec((nn, tm), lambda i: (0, i))],
        out_shape=[jax.ShapeDtypeStruct((t, nn), BF16), jax.ShapeDtypeStruct((nn, t), BF16)],
        compiler_params=_cparams(("parallel",)),
        name="mem_kv",
    )(mem, g, w)


def _flash_kernel(q_ref, qn_ref, k_ref, vt_ref, o_ref, s_ref, c_ref, *, group, dq, dv, tk):
    tq = q_ref.shape[0]
    nk = k_ref.shape[0] // tk
    cols = group * tq

    def stacked(ref):
        if group == 1:
            return ref[...]
        return jnp.concatenate([ref[:, g * dq:(g + 1) * dq] for g in range(group)], axis=0)

    q = stacked(q_ref)

    def scores(qs, j):
        ks = k_ref[pl.ds(pl.multiple_of(j * tk, tk), tk), :]
        s = lax.dot_general(ks, qs, (((1,), (1,)), ((), ())), preferred_element_type=F32)
        return s, jnp.max(s, axis=0, keepdims=True)

    def consume(j, s, s_max, m, l, acc):
        vt = vt_ref[:, pl.ds(pl.multiple_of(j * tk, tk), tk)]
        m_new = jnp.maximum(m, s_max)
        alpha = jnp.exp2(m - m_new)
        p = jnp.exp2(s - m_new)
        l_new = alpha * l + jnp.sum(p, axis=0, keepdims=True)
        acc_new = alpha * acc + _dot(vt, p.astype(BF16))
        return m_new, l_new, acc_new

    state = (jnp.full((1, cols), -jnp.inf, F32), jnp.zeros((1, cols), F32),
             jnp.zeros((dv, cols), F32))
    if nk == 1:
        _, l, acc = consume(0, *scores(q, 0), *state)
    else:
        assert nk % 2 == 0

        @pl.when(pl.program_id(2) == 0)
        def _():
            s0, c0 = scores(q, 0)
            s_ref[0] = s0
            c_ref[...] = c0

        def pair(i, carry):
            m, l, acc, c_even = carry
            j = 2 * i
            s_odd, c_odd = scores(q, j + 1)
            s_ref[1] = s_odd
            m, l, acc = consume(j, s_ref[0], c_even, m, l, acc)
            s_even, c_next = scores(q, j + 2)
            s_ref[0] = s_even
            m, l, acc = consume(j + 1, s_ref[1], c_odd, m, l, acc)
            return m, l, acc, c_next

        m, l, acc, c_even = lax.fori_loop(0, nk // 2 - 1, pair, state + (c_ref[...],))
        s_odd, c_odd = scores(q, nk - 1)
        s_ref[1] = s_odd
        m, l, acc = consume(nk - 2, s_ref[0], c_even, m, l, acc)
        s_next, c_next = scores(stacked(qn_ref), 0)
        s_ref[0] = s_next
        c_ref[...] = c_next
        _, l, acc = consume(nk - 1, s_ref[1], c_odd, m, l, acc)
    o = acc / l
    for g in range(group):
        o_ref[:, g * dv:(g + 1) * dv] = o[:, g * tq:(g + 1) * tq].T.astype(o_ref.dtype)


def _flash(q, k, vt, *, n_kv_heads, group, dq, dv, q_col, k_col, v_row, tq, tk):
    b, s, _ = q.shape
    sk = k.shape[1]
    tk = min(tk, sk)
    nq = s // tq
    q_block = (None, tq, group * dq)
    return pl.pallas_call(
        functools.partial(_flash_kernel, group=group, dq=dq, dv=dv, tk=tk),
        grid=(b, n_kv_heads, nq),
        in_specs=[pl.BlockSpec(q_block, lambda bi, h, qi: (bi, qi, q_col + h)),
                  pl.BlockSpec(q_block, lambda bi, h, qi: (bi, jnp.minimum(qi + 1, nq - 1), q_col + h)),
                  pl.BlockSpec((None, sk, dq), lambda bi, h, qi: (bi, 0, k_col + h)),
                  pl.BlockSpec((dv, sk), lambda bi, h, qi: (v_row + h, bi))],
        out_specs=pl.BlockSpec((None, tq, group * dv), lambda bi, h, qi: (bi, qi, h)),
        out_shape=jax.ShapeDtypeStruct((b, s, n_kv_heads * group * dv), BF16),
        scratch_shapes=[pltpu.VMEM((2, tk, group * tq), F32), pltpu.VMEM((1, group * tq), F32)],
        compiler_params=_cparams(("arbitrary", "arbitrary", "arbitrary")),
        name="flash",
    )(q, q, k, vt)


def _merge_kernel(oa_ref, ob_ref, om_ref, g0_ref, g1_ref, g2_ref, x_ref, wa_ref, wb_ref, wm_ref,
                  wo_ref, gn_ref, h_ref, n2_ref):
    merged = g0_ref[...].astype(F32) * _dot(oa_ref[...], wa_ref[...])
    merged = merged + g1_ref[...].astype(F32) * _dot(ob_ref[...], wb_ref[...])
    merged = merged + g2_ref[...].astype(F32) * _dot(om_ref[...], wm_ref[...])
    h = x_ref[...] + _dot(merged.astype(BF16), wo_ref[...])
    h_ref[...] = h
    n2_ref[...] = _rms(h, gn_ref[...]).astype(n2_ref.dtype)


def _merge(oa, ob, om, gates, x, wa, wb, wm, wo, gn, tm=256):
    t, d = x.shape
    const = lambda i: (0, 0)
    row = lambda a: pl.BlockSpec((tm, a.shape[1]), lambda i: (i, 0))
    wspec = lambda a: pl.BlockSpec(a.shape, const, pipeline_mode=pl.Buffered(1))
    gate = lambda br: pl.BlockSpec((tm, d), lambda i: (i, br))
    return pl.pallas_call(
        _merge_kernel,
        grid=(t // tm,),
        in_specs=[row(oa), row(ob), row(om), gate(0), gate(1), gate(2), row(x),
                  wspec(wa), wspec(wb), wspec(wm), wspec(wo), pl.BlockSpec((1, d), const)],
        out_specs=[pl.BlockSpec((tm, d), lambda i: (i, 0)), pl.BlockSpec((tm, d), lambda i: (i, 0))],
        out_shape=[jax.ShapeDtypeStruct((t, d), F32), jax.ShapeDtypeStruct((t, d), BF16)],
        compiler_params=_cparams(("parallel",)),
        name="merge",
    )(oa, ob, om, gates, gates, gates, x, wa, wb, wm, wo, gn)


def _mlp_kernel(n_ref, h_ref, wu_ref, wd_ref, g_ref, y_ref, *, final_norm):
    f = pl.program_id(1)

    @pl.when(f == 0)
    def _():
        y_ref[...] = h_ref[...]

    u = _dot(n_ref[...], wu_ref[...])
    a = jnp.square(jnp.maximum(u, 0.0)).astype(BF16)
    y_ref[...] += _dot(a, wd_ref[...])

    if final_norm:
        @pl.when(f == pl.num_programs(1) - 1)
        def _():
            y_ref[...] = _rms(y_ref[...], g_ref[...])


def _mlp(n2, h, wu, wd, g, final_norm, tm=512, tf=2048):
    t, d = h.shape
    dff = wu.shape[1]
    return pl.pallas_call(
        functools.partial(_mlp_kernel, final_norm=final_norm),
        grid=(t // tm, dff // tf),
        in_specs=[pl.BlockSpec((tm, d), lambda i, f: (i, 0)),
                  pl.BlockSpec((tm, d), lambda i, f: (i, 0)),
                  pl.BlockSpec((d, tf), lambda i, f: (0, f)),
                  pl.BlockSpec((tf, d), lambda i, f: (f, 0)),
                  pl.BlockSpec((1, d), lambda i, f: (0, 0))],
        out_specs=pl.BlockSpec((tm, d), lambda i, f: (i, 0)),
        out_shape=jax.ShapeDtypeStruct((t, d), F32),
        compiler_params=_cparams(("parallel", "arbitrary")),
        name="mlp",
    )(n2, h, wu, wd, g)


def _half_split_perm(w, dim):
    q = dim // 4
    r_lo, r_hi, c_lo, c_hi = (w[..., i * q:(i + 1) * q] for i in range(4))
    pad = jnp.zeros(w.shape[:-1] + (LANES // 2 - 2 * q,), w.dtype)
    return jnp.concatenate([r_lo, c_lo, pad, r_hi, c_hi, pad], axis=-1)


def _rope_tables(seq, dim):
    n_rows = seq // GRID_W
    h = dim // 2
    inv_freq = ROPE_THETA ** (-jnp.arange(0, h, 2, dtype=F32) / h)

    def cs(n_pos):
        ang = jnp.arange(n_pos, dtype=jnp.int32).astype(F32)[:, None] * inv_freq[None, :]
        return jnp.cos(ang), jnp.sin(ang)

    def lanes(row_lo, col_lo, row_hi, col_hi):
        zr, zc = jnp.zeros_like(row_lo), jnp.zeros_like(col_lo)
        pad = lambda a: jnp.zeros((a.shape[0], LANES // 2 - dim // 2), F32)
        by_row = jnp.concatenate([row_lo, zr, pad(zr), row_hi, zr, pad(zr)], axis=1)
        by_col = jnp.concatenate([zc, col_lo, pad(zc), zc, col_hi, pad(zc)], axis=1)
        return (by_row[:, None, :] + by_col[None, :, :]).reshape(seq, LANES)

    (cr, sr), (cc, sc) = cs(n_rows), cs(GRID_W)
    return lanes(cr, cc, cr, cc), lanes(-sr, -sc, sr, sc)


def _prep_layer(l, w_in, g_qa, g_ka, w_q_b, w_kv_b, w_mem_kv, w_br_a, w_br_b, w_br_m, w_out, w_up,
                w_down):
    d = w_in.shape[1]
    wi = w_in[l].astype(BF16)
    nq, nk = HEADS_A * HEAD_DIM_A, KV_HEADS_A * HEAD_DIM_A
    lat_w = Q_LORA_B + KV_LORA_B + ROPE_DIM_B
    qm_w = HEADS_M * HEAD_DIM_M
    c_lat = nq + 2 * nk
    c_qm, c_gate = c_lat + lat_w, c_lat + lat_w + qm_w
    c_rope = c_lat + Q_LORA_B + KV_LORA_B

    def perm_heads(w, n_heads):
        return _half_split_perm(w.reshape(d, n_heads, HEAD_DIM_A), HEAD_DIM_A).reshape(d, -1)

    w_a = jnp.concatenate([perm_heads(wi[:, :nq], HEADS_A), perm_heads(wi[:, nq:nq + nk], KV_HEADS_A),
                           wi[:, nq + nk:c_lat], wi[:, c_qm:c_gate]], axis=1)
    g_a = jnp.concatenate(
        [jnp.tile(_half_split_perm(g_qa[l], HEAD_DIM_A) * (HEAD_DIM_A ** -0.5 * LOG2E), HEADS_A),
         jnp.tile(_half_split_perm(g_ka[l], HEAD_DIM_A), KV_HEADS_A)])[None]
    w_lat = jnp.concatenate([wi[:, c_lat:c_rope], _half_split_perm(wi[:, c_rope:c_qm], ROPE_DIM_B)],
                            axis=1)
    w_gate = wi[:, c_gate:]
    wq = w_q_b[l].astype(BF16).reshape(Q_LORA_B, HEADS_B, NOPE_DIM_B + ROPE_DIM_B)
    wq = jnp.concatenate([wq[:, :, :NOPE_DIM_B], _half_split_perm(wq[:, :, NOPE_DIM_B:], ROPE_DIM_B)],
                         axis=-1).reshape(Q_LORA_B, HEADS_B * QB_PAD)
    wkv = w_kv_b[l].astype(BF16).reshape(KV_LORA_B, HEADS_B, NOPE_DIM_B + V_DIM_B)
    wkv = jnp.concatenate([wkv[:, :, :NOPE_DIM_B].reshape(KV_LORA_B, -1),
                           wkv[:, :, NOPE_DIM_B:].reshape(KV_LORA_B, -1)], axis=1)
    return dict(w_a=w_a, g_a=g_a, w_lat=w_lat, w_gate=w_gate, wq=wq, wkv=wkv,
                w_mem=w_mem_kv[l].astype(BF16), w_br_a=w_br_a[l].astype(BF16),
                w_br_b=w_br_b[l].astype(BF16), w_br_m=w_br_m[l].astype(BF16),
                w_out=w_out[l].astype(BF16), w_up=w_up[l].astype(BF16),
                w_down=w_down[l].astype(BF16))


def _trunk(x, mem, layers, gains, g_final):
    b, s, d = x.shape
    t = b * s
    n_mem = mem.shape[1]
    tabs_a = _rope_tables(s, HEAD_DIM_A)
    tabs_b = _rope_tables(s, ROPE_DIM_B)
    h = x.reshape(t, d)
    mem2 = mem.reshape(b * n_mem, d)
    depth = len(layers)
    tk = min(1024, s // 2)
    for l, (w, g) in enumerate(zip(layers, gains)):
        n, qk_a, vt_a, q_m = _qkv_a(h, g["g_mix"], w["w_a"], w["g_a"], tabs_a, s)
        qk_a = qk_a.reshape(b, s, -1)
        q_b, k_b, vt_b = _latent(n, w["w_lat"], g["g_cq"], g["g_ckv"], w["wq"], w["wkv"], tabs_b, s)
        gates = _gates(n, w["w_gate"])
        k_m, vt_m = _mem_kv(mem2, g["g_mem"], w["w_mem"])

        o_a = _flash(qk_a, qk_a, vt_a, n_kv_heads=KV_HEADS_A, group=HEADS_A // KV_HEADS_A,
                     dq=HEAD_DIM_A, dv=HEAD_DIM_A, q_col=0, k_col=HEADS_A, v_row=0, tq=512, tk=tk)
        o_b = _flash(q_b.reshape(b, s, -1), k_b.reshape(b, s, -1), vt_b,
                     n_kv_heads=HEADS_B, group=1, dq=QB_PAD, dv=V_DIM_B, q_col=0, k_col=0, v_row=0,
                     tq=min(2048, s // 2), tk=tk)
        o_m = _flash(q_m.reshape(b, s, -1), k_m.reshape(b, n_mem, -1), vt_m, n_kv_heads=HEADS_M,
                     group=1, dq=HEAD_DIM_M, dv=HEAD_DIM_M, q_col=0, k_col=0, v_row=0, tq=2048,
                     tk=n_mem)

        h, n2 = _merge(o_a.reshape(t, -1), o_b.reshape(t, -1), o_m.reshape(t, -1), gates, h,
                       w["w_br_a"], w["w_br_b"], w["w_br_m"], w["w_out"], g["g_mlp"])
        last = l == depth - 1
        h = _mlp(n2, h, w["w_up"], w["w_down"], g_final if last else g["g_mlp"], final_norm=last)
    return h.reshape(b, s, d)


def kernel(x_prompt, x_sample, mem_prompt, mem_sample, g_mix, w_in, g_qa, g_ka, g_cq, w_q_b, g_ckv,
           w_kv_b, g_mem, w_mem_kv, w_br_a, w_br_b, w_br_m, w_out, g_mlp, w_up, w_down, g_final):
    depth = w_in.shape[0]
    layers = [_prep_layer(l, w_in, g_qa, g_ka, w_q_b, w_kv_b, w_mem_kv, w_br_a, w_br_b, w_br_m,
                          w_out, w_up, w_down) for l in range(depth)]
    gains = [dict(g_mix=g_mix[l][None], g_cq=g_cq[l][None], g_ckv=g_ckv[l][None],
                  g_mem=g_mem[l][None], g_mlp=g_mlp[l][None]) for l in range(depth)]
    gf = g_final[None]
    y_prompt = _trunk(x_prompt, mem_prompt, layers, gains, gf)
    y_sample = _trunk(x_sample, mem_sample, layers, gains, gf)
    return (y_prompt, y_sample)
```

```python
import functools
import math

import jax
import jax.numpy as jnp
from jax import lax
from jax.experimental import pallas as pl
from jax.experimental.pallas import tpu as pltpu

F32 = jnp.float32
BF16 = jnp.bfloat16

EPS = 1e-6
ROPE_THETA = 10000.0
GRID_W = 64
LOG2E = math.log2(math.e)

HEADS_A, KV_HEADS_A, HEAD_DIM_A = 8, 2, 128
HEADS_B, Q_LORA_B, KV_LORA_B = 8, 512, 256
NOPE_DIM_B, ROPE_DIM_B, V_DIM_B = 128, 64, 128
HEADS_M, HEAD_DIM_M = 4, 128
N_BRANCH = 3
LANES = 128
QB_PAD = 256

VMEM_LIMIT = 56 * 1024 * 1024


def _cparams(sem):
    return pltpu.CompilerParams(dimension_semantics=sem, vmem_limit_bytes=VMEM_LIMIT)


def _rms(x, g):
    y = x * lax.rsqrt(jnp.mean(x * x, axis=-1, keepdims=True) + EPS)
    return y * g


def _dot(a, b):
    return jnp.dot(a, b, preferred_element_type=F32)


def _rope(x, cos, sin):
    return x * cos + pltpu.roll(x, LANES // 2, 1) * sin


def _proj_kernel(x_ref, gx_ref, w_ref, g_ref, cos_ref, sin_ref, wl_ref, gcq_ref, gckv_ref, wq_ref,
                 wkv_ref, cosb_ref, sinb_ref, n_ref, qk_ref, vt_ref, qm_ref, qb_ref, kb_ref, vtb_ref,
                 *, qm_scale, qb_scale):
    n = _rms(x_ref[...], gx_ref[...]).astype(n_ref.dtype)
    n_ref[...] = n
    z = _dot(n, w_ref[...])
    cos, sin = cos_ref[...], sin_ref[...]
    n_qk = qk_ref.shape[1]
    n_v = vt_ref.shape[0]
    for h in range(n_qk // LANES):
        c = slice(h * LANES, (h + 1) * LANES)
        qk_ref[:, c] = _rope(_rms(z[:, c], g_ref[:, c]), cos, sin).astype(qk_ref.dtype)
    vt_ref[...] = z[:, n_qk:n_qk + n_v].T.astype(vt_ref.dtype)
    qm_ref[...] = (z[:, n_qk + n_v:] * qm_scale).astype(qm_ref.dtype)
    _latent_body(n, wl_ref, gcq_ref, gckv_ref, wq_ref, wkv_ref, cosb_ref, sinb_ref,
                 qb_ref, kb_ref, vtb_ref, qb_scale)


def _proj(x, gx, w, g, tabs_a, wl, gcq, gckv, wq, wkv, tabs_b, seq, tm=512):
    t, d = x.shape
    n_qk = (HEADS_A + KV_HEADS_A) * HEAD_DIM_A
    n_v = KV_HEADS_A * HEAD_DIM_A
    n_qm = HEADS_M * HEAD_DIM_M
    n_b, n_vb = HEADS_B * QB_PAD, HEADS_B * V_DIM_B
    nblk = seq // tm
    const = lambda i: (0, 0)
    whole = lambda a: pl.BlockSpec(a.shape, const, pipeline_mode=pl.Buffered(1))
    tab_spec = pl.BlockSpec((tm, LANES), lambda i: (i % nblk, 0))
    row = lambda width: pl.BlockSpec((tm, width), lambda i: (i, 0))
    col = lambda height: pl.BlockSpec((height, tm), lambda i: (0, i))
    return pl.pallas_call(
        functools.partial(_proj_kernel, qm_scale=HEAD_DIM_M ** -0.5 * LOG2E,
                          qb_scale=(NOPE_DIM_B + ROPE_DIM_B) ** -0.5 * LOG2E),
        grid=(t // tm,),
        in_specs=[row(d), whole(gx), whole(w), whole(g), tab_spec, tab_spec,
                  whole(wl), whole(gcq), whole(gckv), whole(wq), whole(wkv), tab_spec, tab_spec],
        out_specs=[row(d), row(n_qk), col(n_v), row(n_qm), row(n_b), row(n_b), col(n_vb)],
        out_shape=[jax.ShapeDtypeStruct((t, d), BF16),
                   jax.ShapeDtypeStruct((t, n_qk), BF16),
                   jax.ShapeDtypeStruct((n_v, t), BF16),
                   jax.ShapeDtypeStruct((t, n_qm), BF16),
                   jax.ShapeDtypeStruct((t, n_b), BF16),
                   jax.ShapeDtypeStruct((t, n_b), BF16),
                   jax.ShapeDtypeStruct((n_vb, t), BF16)],
        compiler_params=_cparams(("parallel",)),
        name="proj",
    )(x, gx, w, g, *tabs_a, wl, gcq, gckv, wq, wkv, *tabs_b)


def _latent_body(n, wl_ref, gcq_ref, gckv_ref, wq_ref, wkv_ref, cos_ref, sin_ref,
                 q_ref, k_ref, vt_ref, q_scale):
    lat = _dot(n, wl_ref[...])
    cos, sin = cos_ref[...], sin_ref[...]
    cq = _rms(lat[:, :Q_LORA_B], gcq_ref[...]).astype(BF16)
    qb = _dot(cq, wq_ref[...])
    for h in range(HEADS_B):
        c0 = h * QB_PAD
        q_ref[:, c0:c0 + LANES] = (qb[:, c0:c0 + LANES] * q_scale).astype(q_ref.dtype)
        q_ref[:, c0 + LANES:c0 + QB_PAD] = (
            _rope(qb[:, c0 + LANES:c0 + QB_PAD], cos, sin) * q_scale).astype(q_ref.dtype)
    ckv = _rms(lat[:, Q_LORA_B:Q_LORA_B + KV_LORA_B], gckv_ref[...]).astype(BF16)
    kvb = _dot(ckv, wkv_ref[...])
    kpe = _rope(lat[:, Q_LORA_B + KV_LORA_B:], cos, sin).astype(k_ref.dtype)
    for h in range(HEADS_B):
        c0 = h * QB_PAD
        k_ref[:, c0:c0 + LANES] = kvb[:, h * LANES:(h + 1) * LANES].astype(k_ref.dtype)
        k_ref[:, c0 + LANES:c0 + QB_PAD] = kpe
    vt_ref[...] = kvb[:, HEADS_B * NOPE_DIM_B:].T.astype(vt_ref.dtype)


def _gate_kernel(n_ref, w_ref, o_ref):
    o_ref[...] = jax.nn.sigmoid(_dot(n_ref[...], w_ref[...])).astype(o_ref.dtype)


def _gates(n, w, tm=1024, tn=2048):
    t, d = n.shape
    nn = w.shape[1]
    return pl.pallas_call(
        _gate_kernel,
        grid=(nn // tn, t // tm),
        in_specs=[pl.BlockSpec((tm, d), lambda j, i: (i, 0)),
                  pl.BlockSpec((d, tn), lambda j, i: (0, j))],
        out_specs=pl.BlockSpec((tm, tn), lambda j, i: (i, j)),
        out_shape=jax.ShapeDtypeStruct((t, nn), BF16),
        compiler_params=_cparams(("arbitrary", "arbitrary")),
        name="gates",
    )(n, w)


def _mem_kv_kernel(m_ref, g_ref, w_ref, k_ref, vt_ref):
    nm = _rms(m_ref[...], g_ref[...]).astype(BF16)
    z = _dot(nm, w_ref[...])
    nk = k_ref.shape[1]
    k_ref[...] = z[:, :nk].astype(k_ref.dtype)
    vt_ref[...] = z[:, nk:].T.astype(vt_ref.dtype)


def _mem_kv(mem, g, w, tm=256):
    t, d = mem.shape
    nn = w.shape[1] // 2
    return pl.pallas_call(
        _mem_kv_kernel,
        grid=(t // tm,),
        in_specs=[pl.BlockSpec((tm, d), lambda i: (i, 0)),
                  pl.BlockSpec((1, d), lambda i: (0, 0)),
                  pl.BlockSpec((d, 2 * nn), lambda i: (0, 0))],
        out_specs=[pl.BlockSpec((tm, nn), lambda i: (i, 0)),
                   pl.BlockSpec((nn, tm), lambda i: (0, i))],
        out_shape=[jax.ShapeDtypeStruct((t, nn), BF16), jax.ShapeDtypeStruct((nn, t), BF16)],
        compiler_params=_cparams(("parallel",)),
        name="mem_kv",
    )(mem, g, w)


def _flash_kernel(q_ref, qn_ref, k_ref, vt_ref, o_ref, s_ref, c_ref, *, group, dq, dv, tk):
    tq = q_ref.shape[0]
    nk = k_ref.shape[0] // tk
    cols = group * tq

    def stacked(ref):
        if group == 1:
            return ref[...]
        return jnp.concatenate([ref[:, g * dq:(g + 1) * dq] for g in range(group)], axis=0)

    q = stacked(q_ref)

    def scores(qs, j):
        ks = k_ref[pl.ds(pl.multiple_of(j * tk, tk), tk), :]
        s = lax.dot_general(ks, qs, (((1,), (1,)), ((), ())), preferred_element_type=F32)
        return s, jnp.max(s, axis=0, keepdims=True)

    def consume(j, s, s_max, m, l, acc):
        vt = vt_ref[:, pl.ds(pl.multiple_of(j * tk, tk), tk)]
        m_new = jnp.maximum(m, s_max)
        alpha = jnp.exp2(m - m_new)
        p = jnp.exp2(s - m_new)
        l_new = alpha * l + jnp.sum(p, axis=0, keepdims=True)
        acc_new = alpha * acc + _dot(vt, p.astype(BF16))
        return m_new, l_new, acc_new

    state = (jnp.full((1, cols), -jnp.inf, F32), jnp.zeros((1, cols), F32),
             jnp.zeros((dv, cols), F32))
    if nk == 1:
        _, l, acc = consume(0, *scores(q, 0), *state)
    else:
        assert nk % 2 == 0

        @pl.when(pl.program_id(2) == 0)
        def _():
            s0, c0 = scores(q, 0)
            s_ref[0] = s0
            c_ref[...] = c0

        def pair(i, carry):
            m, l, acc, c_even = carry
            j = 2 * i
            s_odd, c_odd = scores(q, j + 1)
            s_ref[1] = s_odd
            m, l, acc = consume(j, s_ref[0], c_even, m, l, acc)
            s_even, c_next = scores(q, j + 2)
            s_ref[0] = s_even
            m, l, acc = consume(j + 1, s_ref[1], c_odd, m, l, acc)
            return m, l, acc, c_next

        m, l, acc, c_even = lax.fori_loop(0, nk // 2 - 1, pair, state + (c_ref[...],))
        s_odd, c_odd = scores(q, nk - 1)
        s_ref[1] = s_odd
        m, l, acc = consume(nk - 2, s_ref[0], c_even, m, l, acc)
        s_next, c_next = scores(stacked(qn_ref), 0)
        s_ref[0] = s_next
        c_ref[...] = c_next
        _, l, acc = consume(nk - 1, s_ref[1], c_odd, m, l, acc)
    o = acc / l
    for g in range(group):
        o_ref[:, g * dv:(g + 1) * dv] = o[:, g * tq:(g + 1) * tq].T.astype(o_ref.dtype)


def _flash(q, k, vt, *, n_kv_heads, group, dq, dv, q_col, k_col, v_row, tq, tk):
    b, s, _ = q.shape
    sk = k.shape[1]
    tk = min(tk, sk)
    nq = s // tq
    q_block = (None, tq, group * dq)
    return pl.pallas_call(
        functools.partial(_flash_kernel, group=group, dq=dq, dv=dv, tk=tk),
        grid=(b, n_kv_heads, nq),
        in_specs=[pl.BlockSpec(q_block, lambda bi, h, qi: (bi, qi, q_col + h)),
                  pl.BlockSpec(q_block, lambda bi, h, qi: (bi, jnp.minimum(qi + 1, nq - 1), q_col + h)),
                  pl.BlockSpec((None, sk, dq), lambda bi, h, qi: (bi, 0, k_col + h)),
                  pl.BlockSpec((dv, sk), lambda bi, h, qi: (v_row + h, bi))],
        out_specs=pl.BlockSpec((None, tq, group * dv), lambda bi, h, qi: (bi, qi, h)),
        out_shape=jax.ShapeDtypeStruct((b, s, n_kv_heads * group * dv), BF16),
        scratch_shapes=[pltpu.VMEM((2, tk, group * tq), F32), pltpu.VMEM((1, group * tq), F32)],
        compiler_params=_cparams(("arbitrary", "arbitrary", "arbitrary")),
        name="flash",
    )(q, q, k, vt)


def _merge_kernel(oa_ref, ob_ref, om_ref, g0_ref, g1_ref, g2_ref, x_ref, wa_ref, wb_ref, wm_ref,
                  wo_ref, gn_ref, h_ref, n2_ref):
    merged = g0_ref[...].astype(F32) * _dot(oa_ref[...], wa_ref[...])
    merged = merged + g1_ref[...].astype(F32) * _dot(ob_ref[...], wb_ref[...])
    merged = merged + g2_ref[...].astype(F32) * _dot(om_ref[...], wm_ref[...])
    h = x_ref[...] + _dot(merged.astype(BF16), wo_ref[...])
    h_ref[...] = h
    n2_ref[...] = _rms(h, gn_ref[...]).astype(n2_ref.dtype)


def _merge(oa, ob, om, gates, x, wa, wb, wm, wo, gn, tm=256):
    t, d = x.shape
    const = lambda i: (0, 0)
    row = lambda a: pl.BlockSpec((tm, a.shape[1]), lambda i: (i, 0))
    wspec = lambda a: pl.BlockSpec(a.shape, const, pipeline_mode=pl.Buffered(1))
    gate = lambda br: pl.BlockSpec((tm, d), lambda i: (i, br))
    return pl.pallas_call(
        _merge_kernel,
        grid=(t // tm,),
        in_specs=[row(oa), row(ob), row(om), gate(0), gate(1), gate(2), row(x),
                  wspec(wa), wspec(wb), wspec(wm), wspec(wo), pl.BlockSpec((1, d), const)],
        out_specs=[pl.BlockSpec((tm, d), lambda i: (i, 0)), pl.BlockSpec((tm, d), lambda i: (i, 0))],
        out_shape=[jax.ShapeDtypeStruct((t, d), F32), jax.ShapeDtypeStruct((t, d), BF16)],
        compiler_params=_cparams(("parallel",)),
        name="merge",
    )(oa, ob, om, gates, gates, gates, x, wa, wb, wm, wo, gn)


def _mlp_kernel(n_ref, h_ref, wu_ref, wd_ref, g_ref, y_ref, *, final_norm):
    f = pl.program_id(1)

    @pl.when(f == 0)
    def _():
        y_ref[...] = h_ref[...]

    u = _dot(n_ref[...], wu_ref[...])
    a = jnp.square(jnp.maximum(u, 0.0)).astype(BF16)
    y_ref[...] += _dot(a, wd_ref[...])

    if final_norm:
        @pl.when(f == pl.num_programs(1) - 1)
        def _():
            y_ref[...] = _rms(y_ref[...], g_ref[...])


def _mlp(n2, h, wu, wd, g, final_norm, tm=512, tf=2048):
    t, d = h.shape
    dff = wu.shape[1]
    return pl.pallas_call(
        functools.partial(_mlp_kernel, final_norm=final_norm),
        grid=(t // tm, dff // tf),
        in_specs=[pl.BlockSpec((tm, d), lambda i, f: (i, 0)),
                  pl.BlockSpec((tm, d), lambda i, f: (i, 0)),
                  pl.BlockSpec((d, tf), lambda i, f: (0, f)),
                  pl.BlockSpec((tf, d), lambda i, f: (f, 0)),
                  pl.BlockSpec((1, d), lambda i, f: (0, 0))],
        out_specs=pl.BlockSpec((tm, d), lambda i, f: (i, 0)),
        out_shape=jax.ShapeDtypeStruct((t, d), F32),
        compiler_params=_cparams(("parallel", "arbitrary")),
        name="mlp",
    )(n2, h, wu, wd, g)


def _half_split_perm(w, dim):
    q = dim // 4
    r_lo, r_hi, c_lo, c_hi = (w[..., i * q:(i + 1) * q] for i in range(4))
    pad = jnp.zeros(w.shape[:-1] + (LANES // 2 - 2 * q,), w.dtype)
    return jnp.concatenate([r_lo, c_lo, pad, r_hi, c_hi, pad], axis=-1)


def _rope_tables(seq, dim):
    n_rows = seq // GRID_W
    h = dim // 2
    inv_freq = ROPE_THETA ** (-jnp.arange(0, h, 2, dtype=F32) / h)

    def cs(n_pos):
        ang = jnp.arange(n_pos, dtype=jnp.int32).astype(F32)[:, None] * inv_freq[None, :]
        return jnp.cos(ang), jnp.sin(ang)

    def lanes(row_lo, col_lo, row_hi, col_hi):
        zr, zc = jnp.zeros_like(row_lo), jnp.zeros_like(col_lo)
        pad = lambda a: jnp.zeros((a.shape[0], LANES // 2 - dim // 2), F32)
        by_row = jnp.concatenate([row_lo, zr, pad(zr), row_hi, zr, pad(zr)], axis=1)
        by_col = jnp.concatenate([zc, col_lo, pad(zc), zc, col_hi, pad(zc)], axis=1)
        return (by_row[:, None, :] + by_col[None, :, :]).reshape(seq, LANES)

    (cr, sr), (cc, sc) = cs(n_rows), cs(GRID_W)
    return lanes(cr, cc, cr, cc), lanes(-sr, -sc, sr, sc)


def _prep_layer(l, w_in, g_qa, g_ka, w_q_b, w_kv_b, w_mem_kv, w_br_a, w_br_b, w_br_m, w_out, w_up,
                w_down):
    d = w_in.shape[1]
    wi = w_in[l].astype(BF16)
    nq, nk = HEADS_A * HEAD_DIM_A, KV_HEADS_A * HEAD_DIM_A
    lat_w = Q_LORA_B + KV_LORA_B + ROPE_DIM_B
    qm_w = HEADS_M * HEAD_DIM_M
    c_lat = nq + 2 * nk
    c_qm, c_gate = c_lat + lat_w, c_lat + lat_w + qm_w
    c_rope = c_lat + Q_LORA_B + KV_LORA_B

    def perm_heads(w, n_heads):
        return _half_split_perm(w.reshape(d, n_heads, HEAD_DIM_A), HEAD_DIM_A).reshape(d, -1)

    w_a = jnp.concatenate([perm_heads(wi[:, :nq], HEADS_A), perm_heads(wi[:, nq:nq + nk], KV_HEADS_A),
                           wi[:, nq + nk:c_lat], wi[:, c_qm:c_gate]], axis=1)
    g_a = jnp.concatenate(
        [jnp.tile(_half_split_perm(g_qa[l], HEAD_DIM_A) * (HEAD_DIM_A ** -0.5 * LOG2E), HEADS_A),
         jnp.tile(_half_split_perm(g_ka[l], HEAD_DIM_A), KV_HEADS_A)])[None]
    w_lat = jnp.concatenate([wi[:, c_lat:c_rope], _half_split_perm(wi[:, c_rope:c_qm], ROPE_DIM_B)],
                            axis=1)
    w_gate = wi[:, c_gate:]
    wq = w_q_b[l].astype(BF16).reshape(Q_LORA_B, HEADS_B, NOPE_DIM_B + ROPE_DIM_B)
    wq = jnp.concatenate([wq[:, :, :NOPE_DIM_B], _half_split_perm(wq[:, :, NOPE_DIM_B:], ROPE_DIM_B)],
                         axis=-1).reshape(Q_LORA_B, HEADS_B * QB_PAD)
    wkv = w_kv_b[l].astype(BF16).reshape(KV_LORA_B, HEADS_B, NOPE_DIM_B + V_DIM_B)
    wkv = jnp.concatenate([wkv[:, :, :NOPE_DIM_B].reshape(KV_LORA_B, -1),
                           wkv[:, :, NOPE_DIM_B:].reshape(KV_LORA_B, -1)], axis=1)
    return dict(w_a=w_a, g_a=g_a, w_lat=w_lat, w_gate=w_gate, wq=wq, wkv=wkv,
                w_mem=w_mem_kv[l].astype(BF16), w_br_a=w_br_a[l].astype(BF16),
                w_br_b=w_br_b[l].astype(BF16), w_br_m=w_br_m[l].astype(BF16),
                w_out=w_out[l].astype(BF16), w_up=w_up[l].astype(BF16),
                w_down=w_down[l].astype(BF16))


def _trunk(x, mem, layers, gains, g_final):
    b, s, d = x.shape
    t = b * s
    n_mem = mem.shape[1]
    tabs_a = _rope_tables(s, HEAD_DIM_A)
    tabs_b = _rope_tables(s, ROPE_DIM_B)
    h = x.reshape(t, d)
    mem2 = mem.reshape(b * n_mem, d)
    depth = len(layers)
    tk = min(1024, s // 2)
    for l, (w, g) in enumerate(zip(layers, gains)):
        n, qk_a, vt_a, q_m, q_b, k_b, vt_b = _proj(
            h, g["g_mix"], w["w_a"], w["g_a"], tabs_a, w["w_lat"], g["g_cq"], g["g_ckv"], w["wq"],
            w["wkv"], tabs_b, s)
        qk_a = qk_a.reshape(b, s, -1)
        gates = _gates(n, w["w_gate"])
        k_m, vt_m = _mem_kv(mem2, g["g_mem"], w["w_mem"])

        o_a = _flash(qk_a, qk_a, vt_a, n_kv_heads=KV_HEADS_A, group=HEADS_A // KV_HEADS_A,
                     dq=HEAD_DIM_A, dv=HEAD_DIM_A, q_col=0, k_col=HEADS_A, v_row=0, tq=512, tk=tk)
        o_b = _flash(q_b.reshape(b, s, -1), k_b.reshape(b, s, -1), vt_b,
                     n_kv_heads=HEADS_B, group=1, dq=QB_PAD, dv=V_DIM_B, q_col=0, k_col=0, v_row=0,
                     tq=min(2048, s // 2), tk=tk)
        o_m = _flash(q_m.reshape(b, s, -1), k_m.reshape(b, n_mem, -1), vt_m, n_kv_heads=HEADS_M,
                     group=1, dq=HEAD_DIM_M, dv=HEAD_DIM_M, q_col=0, k_col=0, v_row=0, tq=2048,
                     tk=n_mem)

        h, n2 = _merge(o_a.reshape(t, -1), o_b.reshape(t, -1), o_m.reshape(t, -1), gates, h,
                       w["w_br_a"], w["w_br_b"], w["w_br_m"], w["w_out"], g["g_mlp"])
        last = l == depth - 1
        h = _mlp(n2, h, w["w_up"], w["w_down"], g_final if last else g["g_mlp"], final_norm=last)
    return h.reshape(b, s, d)


def kernel(x_prompt, x_sample, mem_prompt, mem_sample, g_mix, w_in, g_qa, g_ka, g_cq, w_q_b, g_ckv,
           w_kv_b, g_mem, w_mem_kv, w_br_a, w_br_b, w_br_m, w_out, g_mlp, w_up, w_down, g_final):
    depth = w_in.shape[0]
    layers = [_prep_layer(l, w_in, g_qa, g_ka, w_q_b, w_kv_b, w_mem_kv, w_br_a, w_br_b, w_br_m,
                          w_out, w_up, w_down) for l in range(depth)]
    gains = [dict(g_mix=g_mix[l][None], g_cq=g_cq[l][None], g_ckv=g_ckv[l][None],
                  g_mem=g_mem[l][None], g_mlp=g_mlp[l][None]) for l in range(depth)]
    gf = g_final[None]
    y_prompt = _trunk(x_prompt, mem_prompt, layers, gains, gf)
    y_sample = _trunk(x_sample, mem_sample, layers, gains, gf)
    return (y_prompt, y_sample)
```

```python
import functools
import math

import jax
import jax.numpy as jnp
from jax import lax
from jax.experimental import pallas as pl
from jax.experimental.pallas import tpu as pltpu

F32 = jnp.float32
BF16 = jnp.bfloat16

EPS = 1e-6
ROPE_THETA = 10000.0
GRID_W = 64
LOG2E = math.log2(math.e)

HEADS_A, KV_HEADS_A, HEAD_DIM_A = 8, 2, 128
HEADS_B, Q_LORA_B, KV_LORA_B = 8, 512, 256
NOPE_DIM_B, ROPE_DIM_B, V_DIM_B = 128, 64, 128
HEADS_M, HEAD_DIM_M = 4, 128
N_BRANCH = 3
LANES = 128
QB_PAD = 256

VMEM_LIMIT = 62 * 1024 * 1024


def _cparams(sem):
    return pltpu.CompilerParams(dimension_semantics=sem, vmem_limit_bytes=VMEM_LIMIT)


def _rms(x, g):
    y = x * lax.rsqrt(jnp.mean(x * x, axis=-1, keepdims=True) + EPS)
    return y * g


def _dot(a, b):
    return jnp.dot(a, b, preferred_element_type=F32)


def _rope(x, cos, sin):
    return x * cos + pltpu.roll(x, LANES // 2, 1) * sin


def _proj_kernel(x_ref, gx_ref, w_ref, g_ref, cos_ref, sin_ref, wl_ref, gcq_ref, gckv_ref, wq_ref,
                 wkv_ref, cosb_ref, sinb_ref, n_ref, qk_ref, vt_ref, qm_ref, qb_ref, kb_ref, vtb_ref,
                 *, qm_scale, qb_scale):
    n = _rms(x_ref[...], gx_ref[...]).astype(n_ref.dtype)
    n_ref[...] = n
    z = _dot(n, w_ref[...])
    cos, sin = cos_ref[...], sin_ref[...]
    n_qk = qk_ref.shape[1]
    n_v = vt_ref.shape[0]
    for h in range(n_qk // LANES):
        c = slice(h * LANES, (h + 1) * LANES)
        qk_ref[:, c] = _rope(_rms(z[:, c], g_ref[:, c]), cos, sin).astype(qk_ref.dtype)
    vt_ref[...] = z[:, n_qk:n_qk + n_v].T.astype(vt_ref.dtype)
    qm_ref[...] = (z[:, n_qk + n_v:] * qm_scale).astype(qm_ref.dtype)
    _latent_body(n, wl_ref, gcq_ref, gckv_ref, wq_ref, wkv_ref, cosb_ref, sinb_ref,
                 qb_ref, kb_ref, vtb_ref, qb_scale)


def _proj(x, gx, w, g, tabs_a, wl, gcq, gckv, wq, wkv, tabs_b, seq, tm=512):
    t, d = x.shape
    n_qk = (HEADS_A + KV_HEADS_A) * HEAD_DIM_A
    n_v = KV_HEADS_A * HEAD_DIM_A
    n_qm = HEADS_M * HEAD_DIM_M
    n_b, n_vb = HEADS_B * QB_PAD, HEADS_B * V_DIM_B
    nblk = seq // tm
    const = lambda i: (0, 0)
    whole = lambda a: pl.BlockSpec(a.shape, const, pipeline_mode=pl.Buffered(1))
    tab_spec = pl.BlockSpec((tm, LANES), lambda i: (i % nblk, 0))
    row = lambda width: pl.BlockSpec((tm, width), lambda i: (i, 0))
    col = lambda height: pl.BlockSpec((height, tm), lambda i: (0, i))
    return pl.pallas_call(
        functools.partial(_proj_kernel, qm_scale=HEAD_DIM_M ** -0.5 * LOG2E,
                          qb_scale=(NOPE_DIM_B + ROPE_DIM_B) ** -0.5 * LOG2E),
        grid=(t // tm,),
        in_specs=[row(d), whole(gx), whole(w), whole(g), tab_spec, tab_spec,
                  whole(wl), whole(gcq), whole(gckv), whole(wq), whole(wkv), tab_spec, tab_spec],
        out_specs=[row(d), row(n_qk), col(n_v), row(n_qm), row(n_b), row(n_b), col(n_vb)],
        out_shape=[jax.ShapeDtypeStruct((t, d), BF16),
                   jax.ShapeDtypeStruct((t, n_qk), BF16),
                   jax.ShapeDtypeStruct((n_v, t), BF16),
                   jax.ShapeDtypeStruct((t, n_qm), BF16),
                   jax.ShapeDtypeStruct((t, n_b), BF16),
                   jax.ShapeDtypeStruct((t, n_b), BF16),
                   jax.ShapeDtypeStruct((n_vb, t), BF16)],
        compiler_params=_cparams(("parallel",)),
        name="proj",
    )(x, gx, w, g, *tabs_a, wl, gcq, gckv, wq, wkv, *tabs_b)


def _latent_body(n, wl_ref, gcq_ref, gckv_ref, wq_ref, wkv_ref, cos_ref, sin_ref,
                 q_ref, k_ref, vt_ref, q_scale):
    lat = _dot(n, wl_ref[...])
    cos, sin = cos_ref[...], sin_ref[...]
    cq = _rms(lat[:, :Q_LORA_B], gcq_ref[...]).astype(BF16)
    qb = _dot(cq, wq_ref[...])
    for h in range(HEADS_B):
        c0 = h * QB_PAD
        q_ref[:, c0:c0 + LANES] = (qb[:, c0:c0 + LANES] * q_scale).astype(q_ref.dtype)
        q_ref[:, c0 + LANES:c0 + QB_PAD] = (
            _rope(qb[:, c0 + LANES:c0 + QB_PAD], cos, sin) * q_scale).astype(q_ref.dtype)
    ckv = _rms(lat[:, Q_LORA_B:Q_LORA_B + KV_LORA_B], gckv_ref[...]).astype(BF16)
    kvb = _dot(ckv, wkv_ref[...])
    kpe = _rope(lat[:, Q_LORA_B + KV_LORA_B:], cos, sin).astype(k_ref.dtype)
    for h in range(HEADS_B):
        c0 = h * QB_PAD
        k_ref[:, c0:c0 + LANES] = kvb[:, h * LANES:(h + 1) * LANES].astype(k_ref.dtype)
        k_ref[:, c0 + LANES:c0 + QB_PAD] = kpe
    vt_ref[...] = kvb[:, HEADS_B * NOPE_DIM_B:].T.astype(vt_ref.dtype)


def _gate_kernel(n_ref, w_ref, o_ref):
    o_ref[...] = jax.nn.sigmoid(_dot(n_ref[...], w_ref[...])).astype(o_ref.dtype)


def _gates(n, w, tm=1024, tn=2048):
    t, d = n.shape
    nn = w.shape[1]
    return pl.pallas_call(
        _gate_kernel,
        grid=(nn // tn, t // tm),
        in_specs=[pl.BlockSpec((tm, d), lambda j, i: (i, 0)),
                  pl.BlockSpec((d, tn), lambda j, i: (0, j))],
        out_specs=pl.BlockSpec((tm, tn), lambda j, i: (i, j)),
        out_shape=jax.ShapeDtypeStruct((t, nn), BF16),
        compiler_params=_cparams(("arbitrary", "arbitrary")),
        name="gates",
    )(n, w)


def _mem_kv_kernel(m_ref, g_ref, w_ref, k_ref, vt_ref):
    nm = _rms(m_ref[...], g_ref[...]).astype(BF16)
    z = _dot(nm, w_ref[...])
    nk = k_ref.shape[1]
    k_ref[...] = z[:, :nk].astype(k_ref.dtype)
    vt_ref[...] = z[:, nk:].T.astype(vt_ref.dtype)


def _mem_kv(mem, g, w, tm=256):
    t, d = mem.shape
    nn = w.shape[1] // 2
    return pl.pallas_call(
        _mem_kv_kernel,
        grid=(t // tm,),
        in_specs=[pl.BlockSpec((tm, d), lambda i: (i, 0)),
                  pl.BlockSpec((1, d), lambda i: (0, 0)),
                  pl.BlockSpec((d, 2 * nn), lambda i: (0, 0))],
        out_specs=[pl.BlockSpec((tm, nn), lambda i: (i, 0)),
                   pl.BlockSpec((nn, tm), lambda i: (0, i))],
        out_shape=[jax.ShapeDtypeStruct((t, nn), BF16), jax.ShapeDtypeStruct((nn, t), BF16)],
        compiler_params=_cparams(("parallel",)),
        name="mem_kv",
    )(mem, g, w)


def _flash_kernel(q_ref, qn_ref, k_ref, vt_ref, o_ref, s_ref, c_ref, *, group, dq, dv, tk):
    tq = q_ref.shape[0]
    nk = k_ref.shape[0] // tk
    cols = group * tq

    def stacked(ref):
        if group == 1:
            return ref[...]
        return jnp.concatenate([ref[:, g * dq:(g + 1) * dq] for g in range(group)], axis=0)

    q = stacked(q_ref)

    def scores(qs, j):
        ks = k_ref[pl.ds(pl.multiple_of(j * tk, tk), tk), :]
        s = lax.dot_general(ks, qs, (((1,), (1,)), ((), ())), preferred_element_type=F32)
        return s, jnp.max(s, axis=0, keepdims=True)

    def consume(j, s, s_max, m, l, acc):
        vt = vt_ref[:, pl.ds(pl.multiple_of(j * tk, tk), tk)]
        m_new = jnp.maximum(m, s_max)
        alpha = jnp.exp2(m - m_new)
        p = jnp.exp2(s - m_new)
        l_new = alpha * l + jnp.sum(p, axis=0, keepdims=True)
        acc_new = alpha * acc + _dot(vt, p.astype(BF16))
        return m_new, l_new, acc_new

    state = (jnp.full((1, cols), -jnp.inf, F32), jnp.zeros((1, cols), F32),
             jnp.zeros((dv, cols), F32))
    if nk == 1:
        _, l, acc = consume(0, *scores(q, 0), *state)
    else:
        assert nk % 2 == 0

        @pl.when(pl.program_id(2) == 0)
        def _():
            s0, c0 = scores(q, 0)
            s_ref[0] = s0
            c_ref[...] = c0

        def pair(i, carry):
            m, l, acc, c_even = carry
            j = 2 * i
            s_odd, c_odd = scores(q, j + 1)
            s_ref[1] = s_odd
            m, l, acc = consume(j, s_ref[0], c_even, m, l, acc)
            s_even, c_next = scores(q, j + 2)
            s_ref[0] = s_even
            m, l, acc = consume(j + 1, s_ref[1], c_odd, m, l, acc)
            return m, l, acc, c_next

        m, l, acc, c_even = lax.fori_loop(0, nk // 2 - 1, pair, state + (c_ref[...],))
        s_odd, c_odd = scores(q, nk - 1)
        s_ref[1] = s_odd
        m, l, acc = consume(nk - 2, s_ref[0], c_even, m, l, acc)
        s_next, c_next = scores(stacked(qn_ref), 0)
        s_ref[0] = s_next
        c_ref[...] = c_next
        _, l, acc = consume(nk - 1, s_ref[1], c_odd, m, l, acc)
    o = acc / l
    for g in range(group):
        o_ref[:, g * dv:(g + 1) * dv] = o[:, g * tq:(g + 1) * tq].T.astype(o_ref.dtype)


def _flash(q, k, vt, *, n_kv_heads, group, dq, dv, q_col, k_col, v_row, tq, tk):
    b, s, _ = q.shape
    sk = k.shape[1]
    tk = min(tk, sk)
    nq = s // tq
    q_block = (None, tq, group * dq)
    return pl.pallas_call(
        functools.partial(_flash_kernel, group=group, dq=dq, dv=dv, tk=tk),
        grid=(b, n_kv_heads, nq),
        in_specs=[pl.BlockSpec(q_block, lambda bi, h, qi: (bi, qi, q_col + h)),
                  pl.BlockSpec(q_block, lambda bi, h, qi: (bi, jnp.minimum(qi + 1, nq - 1), q_col + h)),
                  pl.BlockSpec((None, sk, dq), lambda bi, h, qi: (bi, 0, k_col + h)),
                  pl.BlockSpec((dv, sk), lambda bi, h, qi: (v_row + h, bi))],
        out_specs=pl.BlockSpec((None, tq, group * dv), lambda bi, h, qi: (bi, qi, h)),
        out_shape=jax.ShapeDtypeStruct((b, s, n_kv_heads * group * dv), BF16),
        scratch_shapes=[pltpu.VMEM((2, tk, group * tq), F32), pltpu.VMEM((1, group * tq), F32)],
        compiler_params=_cparams(("arbitrary", "arbitrary", "arbitrary")),
        name="flash",
    )(q, q, k, vt)


def _merge_kernel(oa_ref, ob_ref, om_ref, g0_ref, g1_ref, g2_ref, x_ref, wa_ref, wb_ref, wm_ref,
                  wo_ref, gn_ref, h_ref, n2_ref):
    merged = g0_ref[...].astype(F32) * _dot(oa_ref[...], wa_ref[...])
    merged = merged + g1_ref[...].astype(F32) * _dot(ob_ref[...], wb_ref[...])
    merged = merged + g2_ref[...].astype(F32) * _dot(om_ref[...], wm_ref[...])
    h = x_ref[...] + _dot(merged.astype(BF16), wo_ref[...])
    h_ref[...] = h
    n2_ref[...] = _rms(h, gn_ref[...]).astype(n2_ref.dtype)


def _merge(oa, ob, om, gates, x, wa, wb, wm, wo, gn, tm=512):
    t, d = x.shape
    const = lambda i: (0, 0)
    row = lambda a: pl.BlockSpec((tm, a.shape[1]), lambda i: (i, 0))
    wspec = lambda a: pl.BlockSpec(a.shape, const, pipeline_mode=pl.Buffered(1))
    gate = lambda br: pl.BlockSpec((tm, d), lambda i: (i, br))
    return pl.pallas_call(
        _merge_kernel,
        grid=(t // tm,),
        in_specs=[row(oa), row(ob), row(om), gate(0), gate(1), gate(2), row(x),
                  wspec(wa), wspec(wb), wspec(wm), wspec(wo), pl.BlockSpec((1, d), const)],
        out_specs=[pl.BlockSpec((tm, d), lambda i: (i, 0)), pl.BlockSpec((tm, d), lambda i: (i, 0))],
        out_shape=[jax.ShapeDtypeStruct((t, d), F32), jax.ShapeDtypeStruct((t, d), BF16)],
        compiler_params=_cparams(("parallel",)),
        name="merge",
    )(oa, ob, om, gates, gates, gates, x, wa, wb, wm, wo, gn)


def _mlp_kernel(n_ref, h_ref, wu_ref, wd_ref, g_ref, y_ref, *, final_norm):
    f = pl.program_id(1)

    @pl.when(f == 0)
    def _():
        y_ref[...] = h_ref[...]

    u = _dot(n_ref[...], wu_ref[...])
    a = jnp.square(jnp.maximum(u, 0.0)).astype(BF16)
    y_ref[...] += _dot(a, wd_ref[...])

    if final_norm:
        @pl.when(f == pl.num_programs(1) - 1)
        def _():
            y_ref[...] = _rms(y_ref[...], g_ref[...])


def _mlp(n2, h, wu, wd, g, final_norm, tm=512, tf=2048):
    t, d = h.shape
    dff = wu.shape[1]
    return pl.pallas_call(
        functools.partial(_mlp_kernel, final_norm=final_norm),
        grid=(t // tm, dff // tf),
        in_specs=[pl.BlockSpec((tm, d), lambda i, f: (i, 0)),
                  pl.BlockSpec((tm, d), lambda i, f: (i, 0)),
                  pl.BlockSpec((d, tf), lambda i, f: (0, f)),
                  pl.BlockSpec((tf, d), lambda i, f: (f, 0)),
                  pl.BlockSpec((1, d), lambda i, f: (0, 0))],
        out_specs=pl.BlockSpec((tm, d), lambda i, f: (i, 0)),
        out_shape=jax.ShapeDtypeStruct((t, d), F32),
        compiler_params=_cparams(("parallel", "arbitrary")),
        name="mlp",
    )(n2, h, wu, wd, g)


def _half_split_perm(w, dim):
    q = dim // 4
    r_lo, r_hi, c_lo, c_hi = (w[..., i * q:(i + 1) * q] for i in range(4))
    pad = jnp.zeros(w.shape[:-1] + (LANES // 2 - 2 * q,), w.dtype)
    return jnp.concatenate([r_lo, c_lo, pad, r_hi, c_hi, pad], axis=-1)


def _rope_tables(seq, dim):
    n_rows = seq // GRID_W
    h = dim // 2
    inv_freq = ROPE_THETA ** (-jnp.arange(0, h, 2, dtype=F32) / h)

    def cs(n_pos):
        ang = jnp.arange(n_pos, dtype=jnp.int32).astype(F32)[:, None] * inv_freq[None, :]
        return jnp.cos(ang), jnp.sin(ang)

    def lanes(row_lo, col_lo, row_hi, col_hi):
        zr, zc = jnp.zeros_like(row_lo), jnp.zeros_like(col_lo)
        pad = lambda a: jnp.zeros((a.shape[0], LANES // 2 - dim // 2), F32)
        by_row = jnp.concatenate([row_lo, zr, pad(zr), row_hi, zr, pad(zr)], axis=1)
        by_col = jnp.concatenate([zc, col_lo, pad(zc), zc, col_hi, pad(zc)], axis=1)
        return (by_row[:, None, :] + by_col[None, :, :]).reshape(seq, LANES)

    (cr, sr), (cc, sc) = cs(n_rows), cs(GRID_W)
    return lanes(cr, cc, cr, cc), lanes(-sr, -sc, sr, sc)


def _prep_layer(l, w_in, g_qa, g_ka, w_q_b, w_kv_b, w_mem_kv, w_br_a, w_br_b, w_br_m, w_out, w_up,
                w_down):
    d = w_in.shape[1]
    wi = w_in[l].astype(BF16)
    nq, nk = HEADS_A * HEAD_DIM_A, KV_HEADS_A * HEAD_DIM_A
    lat_w = Q_LORA_B + KV_LORA_B + ROPE_DIM_B
    qm_w = HEADS_M * HEAD_DIM_M
    c_lat = nq + 2 * nk
    c_qm, c_gate = c_lat + lat_w, c_lat + lat_w + qm_w
    c_rope = c_lat + Q_LORA_B + KV_LORA_B

    def perm_heads(w, n_heads):
        return _half_split_perm(w.reshape(d, n_heads, HEAD_DIM_A), HEAD_DIM_A).reshape(d, -1)

    w_a = jnp.concatenate([perm_heads(wi[:, :nq], HEADS_A), perm_heads(wi[:, nq:nq + nk], KV_HEADS_A),
                           wi[:, nq + nk:c_lat], wi[:, c_qm:c_gate]], axis=1)
    g_a = jnp.concatenate(
        [jnp.tile(_half_split_perm(g_qa[l], HEAD_DIM_A) * (HEAD_DIM_A ** -0.5 * LOG2E), HEADS_A),
         jnp.tile(_half_split_perm(g_ka[l], HEAD_DIM_A), KV_HEADS_A)])[None]
    w_lat = jnp.concatenate([wi[:, c_lat:c_rope], _half_split_perm(wi[:, c_rope:c_qm], ROPE_DIM_B)],
                            axis=1)
    w_gate = wi[:, c_gate:]
    wq = w_q_b[l].astype(BF16).reshape(Q_LORA_B, HEADS_B, NOPE_DIM_B + ROPE_DIM_B)
    wq = jnp.concatenate([wq[:, :, :NOPE_DIM_B], _half_split_perm(wq[:, :, NOPE_DIM_B:], ROPE_DIM_B)],
                         axis=-1).reshape(Q_LORA_B, HEADS_B * QB_PAD)
    wkv = w_kv_b[l].astype(BF16).reshape(KV_LORA_B, HEADS_B, NOPE_DIM_B + V_DIM_B)
    wkv = jnp.concatenate([wkv[:, :, :NOPE_DIM_B].reshape(KV_LORA_B, -1),
                           wkv[:, :, NOPE_DIM_B:].reshape(KV_LORA_B, -1)], axis=1)
    return dict(w_a=w_a, g_a=g_a, w_lat=w_lat, w_gate=w_gate, wq=wq, wkv=wkv,
                w_mem=w_mem_kv[l].astype(BF16), w_br_a=w_br_a[l].astype(BF16),
                w_br_b=w_br_b[l].astype(BF16), w_br_m=w_br_m[l].astype(BF16),
                w_out=w_out[l].astype(BF16), w_up=w_up[l].astype(BF16),
                w_down=w_down[l].astype(BF16))


def _trunk(x, mem, layers, gains, g_final):
    b, s, d = x.shape
    t = b * s
    n_mem = mem.shape[1]
    tabs_a = _rope_tables(s, HEAD_DIM_A)
    tabs_b = _rope_tables(s, ROPE_DIM_B)
    h = x.reshape(t, d)
    mem2 = mem.reshape(b * n_mem, d)
    depth = len(layers)
    tk = min(1024, s // 2)
    for l, (w, g) in enumerate(zip(layers, gains)):
        n, qk_a, vt_a, q_m, q_b, k_b, vt_b = _proj(
            h, g["g_mix"], w["w_a"], w["g_a"], tabs_a, w["w_lat"], g["g_cq"], g["g_ckv"], w["wq"],
            w["wkv"], tabs_b, s)
        qk_a = qk_a.reshape(b, s, -1)
        gates = _gates(n, w["w_gate"])
        k_m, vt_m = _mem_kv(mem2, g["g_mem"], w["w_mem"])

        o_a = _flash(qk_a, qk_a, vt_a, n_kv_heads=KV_HEADS_A, group=HEADS_A // KV_HEADS_A,
                     dq=HEAD_DIM_A, dv=HEAD_DIM_A, q_col=0, k_col=HEADS_A, v_row=0, tq=512, tk=tk)
        o_b = _flash(q_b.reshape(b, s, -1), k_b.reshape(b, s, -1), vt_b,
                     n_kv_heads=HEADS_B, group=1, dq=QB_PAD, dv=V_DIM_B, q_col=0, k_col=0, v_row=0,
                     tq=min(2048, s // 2), tk=tk)
        o_m = _flash(q_m.reshape(b, s, -1), k_m.reshape(b, n_mem, -1), vt_m, n_kv_heads=HEADS_M,
                     group=1, dq=HEAD_DIM_M, dv=HEAD_DIM_M, q_col=0, k_col=0, v_row=0, tq=2048,
                     tk=n_mem)

        h, n2 = _merge(o_a.reshape(t, -1), o_b.reshape(t, -1), o_m.reshape(t, -1), gates, h,
                       w["w_br_a"], w["w_br_b"], w["w_br_m"], w["w_out"], g["g_mlp"])
        last = l == depth - 1
        h = _mlp(n2, h, w["w_up"], w["w_down"], g_final if last else g["g_mlp"], final_norm=last)
    return h.reshape(b, s, d)


def kernel(x_prompt, x_sample, mem_prompt, mem_sample, g_mix, w_in, g_qa, g_ka, g_cq, w_q_b, g_ckv,
           w_kv_b, g_mem, w_mem_kv, w_br_a, w_br_b, w_br_m, w_out, g_mlp, w_up, w_down, g_final):
    depth = w_in.shape[0]
    layers = [_prep_layer(l, w_in, g_qa, g_ka, w_q_b, w_kv_b, w_mem_kv, w_br_a, w_br_b, w_br_m,
                          w_out, w_up, w_down) for l in range(depth)]
    gains = [dict(g_mix=g_mix[l][None], g_cq=g_cq[l][None], g_ckv=g_ckv[l][None],
                  g_mem=g_mem[l][None], g_mlp=g_mlp[l][None]) for l in range(depth)]
    gf = g_final[None]
    y_prompt = _trunk(x_prompt, mem_prompt, layers, gains, gf)
    y_sample = _trunk(x_sample, mem_sample, layers, gains, gf)
    return (y_prompt, y_sample)
```
